```python
import numpy as np
import jax
import jax.numpy as jnp
from jax import lax

D_MODEL = 2048
BATCH = 1
SEQ = 8192
DEPTH = 1
DEC_BATCH = 128
DEC_SEQ = 8
PAST_LEN = 2048
PAGE_SIZE = 128

RW_HEAD = 64
RW_HEADS = D_MODEL // RW_HEAD
RW_WIDTH = RW_HEADS * RW_HEAD
DECAY_LORA = 96
ICLR_LORA = 96
GATE_LORA = 256
LNX_EPS = 64e-5
RW_COLS = (RW_WIDTH, RW_WIDTH, RW_WIDTH, DECAY_LORA, ICLR_LORA, GATE_LORA)
SHIFT_WIDTH = sum(RW_COLS)
ATT_HEAD_DIM = 128
ATT_HEADS = D_MODEL // ATT_HEAD_DIM
KV_HEADS = 4
ATT_WIDTH = ATT_HEADS * ATT_HEAD_DIM
KV_WIDTH = KV_HEADS * ATT_HEAD_DIM
IDX_HEADS = 16
IDX_DIM = 64
TOPK_MAX = 256
Q_BLOCK = 128
DSA_COLS = (ATT_WIDTH, KV_WIDTH, KV_WIDTH, IDX_HEADS * IDX_DIM, IDX_DIM, IDX_HEADS)
DSA_WIDTH = sum(DSA_COLS)
IN_COLS = SHIFT_WIDTH + DSA_WIDTH + 2 * D_MODEL
D_FF = 11 * D_MODEL // 4
CONV_W = 3
NORM_EPS = 1e-6

kernel_name = 'rwkv7_dsa_gated_hybrid_step'


def split_cols(z, sizes):
    cuts = [int(c) for c in np.cumsum(sizes)[:-1]]
    return jnp.split(z, cuts, axis=-1)


def rms_norm(x, g):
    xf = x.astype(jnp.float32)
    y = xf * lax.rsqrt(jnp.mean(xf * xf, axis=-1, keepdims=True) + NORM_EPS)
    return (y * g.astype(jnp.float32)).astype(x.dtype)


def rwkv7_scan(S0, r, w, k, v, a, b):
    def step(S, inp):
        r_t, w_t, k_t, v_t, a_t, b_t = inp
        sa = jnp.einsum('bhvk,bhk->bhv', S, a_t)
        S = S * w_t[:, :, None, :] + sa[..., None] * b_t[:, :, None, :] + v_t[..., None] * k_t[:, :, None, :]
        return S, jnp.einsum('bhvk,bhk->bhv', S, r_t)
    xs = tuple(jnp.swapaxes(t, 0, 1) for t in (r, w, k, v, a, b))
    S, out = lax.scan(step, S0, xs)
    return jnp.swapaxes(out, 0, 1), S


def rwkv7_branch(p_rw, prev_row, S0, P):
    B, T, _ = p_rw.shape
    f32 = jnp.float32
    p_prev = jnp.concatenate([prev_row[:, None, :].astype(p_rw.dtype), p_rw[:, :-1]], axis=1)
    z = p_rw + (p_prev - p_rw) * P['shift_mu']
    r, k, v, wd, ad, gd = split_cols(z, RW_COLS)
    w_log = -jax.nn.softplus(-(P['w0'] + jnp.tanh(wd) @ P['w_decay_up']).astype(f32)) - 0.5
    decay = jnp.exp(-jnp.exp(w_log))
    a = jax.nn.sigmoid(P['a0'] + ad @ P['w_a_up'])
    g = jax.nn.sigmoid(gd) @ P['w_g_up']
    heads = lambda t: t.astype(f32).reshape(B, T, RW_HEADS, RW_HEAD)
    kk = heads(k * P['k_k'])
    kk = kk / jnp.maximum(jnp.sqrt(jnp.sum(kk * kk, axis=-1, keepdims=True)), 1e-12)
    k = k * (1 + (a - 1) * P['k_a'])
    rh, kh, vh, ah = heads(r), heads(k), heads(v), heads(a)
    out, S = rwkv7_scan(S0.astype(f32), rh, heads(decay), kh, vh, -kk, kk * ah)
    mu = jnp.mean(out, axis=-1, keepdims=True)
    var = jnp.mean(jnp.square(out - mu), axis=-1, keepdims=True)
    on = ((out - mu) * lax.rsqrt(var + LNX_EPS)).reshape(B, T, RW_WIDTH) * P['lnx_w'] + P['lnx_b']
    bonus = jnp.sum(rh * kh * P['r_k'].astype(f32), axis=-1, keepdims=True) * vh
    o = (on + bonus.reshape(B, T, RW_WIDTH)) * g
    return o.astype(p_rw.dtype), S.astype(S0.dtype), p_rw[:, -1]


def indexer_scores(qi, wi, kidx):
    dots = jnp.einsum('thd,sd->ths', qi.astype(jnp.float32), kidx.astype(jnp.float32)) * (IDX_DIM ** -0.5)
    return jnp.einsum('ths,th->ts', jax.nn.relu(dots), wi.astype(jnp.float32) * (IDX_HEADS ** -0.5))


def gathered_attention(q, k_sel, v_sel, valid):
    T = q.shape[0]
    qg = q.astype(jnp.float32).reshape(T, KV_HEADS, ATT_HEADS // KV_HEADS, ATT_HEAD_DIM)
    logits = jnp.einsum('tngd,tknd->tngk', qg, k_sel.astype(jnp.float32)) * (ATT_HEAD_DIM ** -0.5)
    logits = jnp.where(valid[:, None, None, :], logits, -jnp.inf)
    p = jax.nn.softmax(logits, axis=-1)
    o = jnp.einsum('tngk,tknd->tngd', p, v_sel.astype(jnp.float32))
    return o.reshape(T, ATT_WIDTH)


def dsa_prompt(q, k, v, qi, kidx, wi):
    B, S = q.shape[0], q.shape[1]
    topk = min(TOPK_MAX, S // 4)
    key_pos = jnp.arange(S)

    def block(i):
        start = i * Q_BLOCK
        sl = lambda t: lax.dynamic_slice_in_dim(t, start, Q_BLOCK, axis=1)
        pos = start + jnp.arange(Q_BLOCK)
        allowed = key_pos[None, :] <= pos[:, None]

        def per_seq(qb, qib, wib, kb, vb, kib):
            sc = jnp.where(allowed, indexer_scores(qib, wib, kib), -jnp.inf)
            idx = lax.top_k(sc, topk)[1]
            return gathered_attention(qb, kb[idx], vb[idx], idx <= pos[:, None])
        return jax.vmap(per_seq)(sl(q), sl(qi), sl(wi), k, v, kidx)

    out = lax.map(block, jnp.arange(S // Q_BLOCK))
    return jnp.swapaxes(out, 0, 1).reshape(B, S, ATT_WIDTH)


def dsa_sample(q, k, v, qi, kidx, wi, cache_k, cache_v, cache_kidx, page_table):
    T = q.shape[1]
    past_len = page_table.shape[1] * PAGE_SIZE
    L = past_len + T
    topk = min(TOPK_MAX, L // 4)
    pos = past_len + jnp.arange(T)
    allowed = jnp.arange(L)[None, :] <= pos[:, None]

    def per_seq(args):
        qb, kb, vb, qib, kib, wib, pages = args
        past_kidx = cache_kidx[pages].reshape(past_len, IDX_DIM)
        all_kidx = jnp.concatenate([past_kidx, kib.astype(past_kidx.dtype)], axis=0)
        sc = jnp.where(allowed, indexer_scores(qib, wib, all_kidx), -jnp.inf)
        idx = lax.top_k(sc, topk)[1]
        is_new = (idx >= past_len)[..., None, None]
        pidx = jnp.minimum(idx, past_len - 1)
        phys = pages[pidx // PAGE_SIZE]
        off = pidx % PAGE_SIZE
        nidx = jnp.clip(idx - past_len, 0, T - 1)
        k_sel = jnp.where(is_new, kb[nidx].astype(cache_k.dtype), cache_k[phys, off])
        v_sel = jnp.where(is_new, vb[nidx].astype(cache_v.dtype), cache_v[phys, off])
        return gathered_attention(qb, k_sel, v_sel, idx <= pos[:, None])

    return lax.map(per_seq, (q, k, v, qi, kidx, wi, page_table))


def layer(x, c, shift_prev, S0, conv_prev, attend, P):
    B, T, _ = x.shape
    mod = jax.nn.silu(c) @ P['w_ada'] + P['b_ada']
    sh1, sc1, g1, sh2, sc2, g2 = [m[:, None, :] for m in jnp.split(mod, 6, axis=-1)]
    h = rms_norm(x, P['norm1_g']) * (1 + sc1) + sh1
    z = h @ P['w_in']
    p_rw, z_dsa, z_gate = split_cols(z, (SHIFT_WIDTH, DSA_WIDTH, 2 * D_MODEL))
    o_rw, S_new, shift_new = rwkv7_branch(p_rw, shift_prev, S0, P)
    q, k, v, qi, ki, wi = split_cols(z_dsa, DSA_COLS)
    q = rms_norm(q.reshape(B, T, ATT_HEADS, ATT_HEAD_DIM), P['q_norm_g'])
    k = rms_norm(k.reshape(B, T, KV_HEADS, ATT_HEAD_DIM), P['k_norm_g'])
    v = v.reshape(B, T, KV_HEADS, ATT_HEAD_DIM)
    qi = qi.reshape(B, T, IDX_HEADS, IDX_DIM)
    o_att = attend(q, k, v, qi, ki, wi).astype(x.dtype)
    gate_a, gate_b = jnp.split(jax.nn.sigmoid(z_gate), 2, axis=-1)
    mixed = gate_a * (o_rw @ P['w_proj_a']) + gate_b * (o_att @ P['w_proj_b'])
    x = x + g1 * (mixed @ P['w_out'])
    h2 = rms_norm(x, P['norm2_g']) * (1 + sc2) + sh2
    u_gate, u_val = jnp.split(h2 @ P['w_ffn_up'], 2, axis=-1)
    padded = jnp.concatenate([conv_prev.astype(u_gate.dtype), u_gate], axis=1)
    acc = P['conv_b']
    for j in range(CONV_W):
        acc = acc + padded[:, j:j + T] * P['conv_w'][j]
    ffn = (jax.nn.silu(acc) * u_val) @ P['w_ffn_down']
    x = x + g2 * ffn
    return x, k, v, ki, S_new, shift_new, padded[:, padded.shape[1] - (CONV_W - 1):]


def setup_inputs(seed: int = 0) -> dict:
    key = jax.random.key(seed)
    ks = iter(jax.random.split(key, 48))
    nrm = lambda shape, s: jax.random.normal(next(ks), shape, jnp.float32) * s
    unif = lambda shape, lo, hi: jax.random.uniform(next(ks), shape, jnp.float32, lo, hi)
    Ld = (DEPTH,)
    n_pages = PAST_LEN // PAGE_SIZE
    used = DEC_BATCH * n_pages
    n_pool = used + max(1, used // 4)
    page_table = jax.random.permutation(next(ks), n_pool)[:used].reshape(DEC_BATCH, n_pages).astype(jnp.int32)
    return {
        'x_prompt': nrm((BATCH, SEQ, D_MODEL), 1.0),
        'x_sample': nrm((DEC_BATCH, DEC_SEQ, D_MODEL), 1.0),
        'c_prompt': nrm((BATCH, D_MODEL), 1.0),
        'c_sample': nrm((DEC_BATCH, D_MODEL), 1.0),
        'cache_k': nrm(Ld + (n_pool, PAGE_SIZE, KV_HEADS, ATT_HEAD_DIM), 1.0),
        'cache_v': nrm(Ld + (n_pool, PAGE_SIZE, KV_HEADS, ATT_HEAD_DIM), 1.0),
        'cache_kidx': nrm(Ld + (n_pool, PAGE_SIZE, IDX_DIM), 1.0),
        'page_table': page_table,
        'state_rwkv': nrm(Ld + (DEC_BATCH, RW_HEADS, RW_HEAD, RW_HEAD), 0.3),
        'state_shift': nrm(Ld + (DEC_BATCH, SHIFT_WIDTH), 1.0),
        'state_conv': nrm(Ld + (DEC_BATCH, CONV_W - 1, D_FF), 1.0),
        'w_ada': nrm(Ld + (D_MODEL, 6 * D_MODEL), 0.5 * D_MODEL ** -0.5),
        'b_ada': nrm(Ld + (6 * D_MODEL,), 0.01),
        'norm1_g': 1.0 + nrm(Ld + (D_MODEL,), 0.1),
        'norm2_g': 1.0 + nrm(Ld + (D_MODEL,), 0.1),
        'w_in': nrm(Ld + (D_MODEL, IN_COLS), D_MODEL ** -0.5),
        'shift_mu': unif(Ld + (SHIFT_WIDTH,), 0.0, 1.0),
        'w0': unif(Ld + (RW_WIDTH,), -5.0, -1.0),
        'w_decay_up': nrm(Ld + (DECAY_LORA, RW_WIDTH), 0.5 * DECAY_LORA ** -0.5),
        'a0': nrm(Ld + (RW_WIDTH,), 0.5),
        'w_a_up': nrm(Ld + (ICLR_LORA, RW_WIDTH), 0.5 * ICLR_LORA ** -0.5),
        'w_g_up': nrm(Ld + (GATE_LORA, RW_WIDTH), GATE_LORA ** -0.5),
        'k_k': 0.85 + nrm(Ld + (RW_WIDTH,), 0.1),
        'k_a': 1.0 + nrm(Ld + (RW_WIDTH,), 0.1),
        'r_k': nrm(Ld + (RW_HEADS, RW_HEAD), 0.1),
        'lnx_w': 1.0 + nrm(Ld + (RW_WIDTH,), 0.1),
        'lnx_b': nrm(Ld + (RW_WIDTH,), 0.01),
        'q_norm_g': 1.0 + nrm(Ld + (ATT_HEAD_DIM,), 0.1),
        'k_norm_g': 1.0 + nrm(Ld + (ATT_HEAD_DIM,), 0.1),
        'w_proj_a': nrm(Ld + (RW_WIDTH, D_MODEL), RW_WIDTH ** -0.5),
        'w_proj_b': nrm(Ld + (ATT_WIDTH, D_MODEL), ATT_WIDTH ** -0.5),
        'w_out': nrm(Ld + (D_MODEL, D_MODEL), D_MODEL ** -0.5),
        'w_ffn_up': nrm(Ld + (D_MODEL, 2 * D_FF), D_MODEL ** -0.5),
        'conv_w': nrm(Ld + (CONV_W, D_FF), CONV_W ** -0.5),
        'conv_b': nrm(Ld + (D_FF,), 0.01),
        'w_ffn_down': nrm(Ld + (D_FF, D_MODEL), D_FF ** -0.5),
    }


def reference(x_prompt, x_sample, c_prompt, c_sample, cache_k, cache_v, cache_kidx, page_table,
              state_rwkv, state_shift, state_conv, w_ada, b_ada, norm1_g, norm2_g, w_in, shift_mu,
              w0, w_decay_up, a0, w_a_up, w_g_up, k_k, k_a, r_k, lnx_w, lnx_b, q_norm_g, k_norm_g,
              w_proj_a, w_proj_b, w_out, w_ffn_up, conv_w, conv_b, w_ffn_down):
    Bp = x_prompt.shape[0]
    dt = x_prompt.dtype
    yp, ys = x_prompt, x_sample
    st_p, st_s = [], []
    for l in range(DEPTH):
        P = {'w_ada': w_ada[l], 'b_ada': b_ada[l], 'norm1_g': norm1_g[l], 'norm2_g': norm2_g[l],
             'w_in': w_in[l], 'shift_mu': shift_mu[l], 'w0': w0[l], 'w_decay_up': w_decay_up[l],
             'a0': a0[l], 'w_a_up': w_a_up[l], 'w_g_up': w_g_up[l], 'k_k': k_k[l], 'k_a': k_a[l],
             'r_k': r_k[l], 'lnx_w': lnx_w[l], 'lnx_b': lnx_b[l], 'q_norm_g': q_norm_g[l],
             'k_norm_g': k_norm_g[l], 'w_proj_a': w_proj_a[l], 'w_proj_b': w_proj_b[l], 'w_out': w_out[l],
             'w_ffn_up': w_ffn_up[l], 'conv_w': conv_w[l], 'conv_b': conv_b[l], 'w_ffn_down': w_ffn_down[l]}
        yp, *sp = layer(yp, c_prompt,
                        jnp.zeros((Bp, SHIFT_WIDTH), dt),
                        jnp.zeros((Bp, RW_HEADS, RW_HEAD, RW_HEAD), jnp.float32),
                        jnp.zeros((Bp, CONV_W - 1, D_FF), dt),
                        dsa_prompt, P)
        attend_s = lambda q, k, v, qi, ki, wi, l=l: dsa_sample(q, k, v, qi, ki, wi, cache_k[l], cache_v[l],
                                                              cache_kidx[l], page_table)
        ys, *ss = layer(ys, c_sample, state_shift[l], state_rwkv[l], state_conv[l], attend_s, P)
        st_p.append(sp)
        st_s.append(ss)
    stk = lambda sts, i: jnp.stack([s[i] for s in sts], axis=0)
    return (yp, ys,
            stk(st_p, 0), stk(st_p, 1), stk(st_p, 2), stk(st_p, 3), stk(st_p, 4), stk(st_p, 5),
            stk(st_s, 0), stk(st_s, 1), stk(st_s, 2), stk(st_s, 3), stk(st_s, 4), stk(st_s, 5))
```

```python
import functools

import numpy as np
import jax
import jax.numpy as jnp
from jax import lax
from jax.experimental import pallas as pl
from jax.experimental.pallas import tpu as pltpu

F32 = jnp.float32
BF16 = jnp.bfloat16
I32 = jnp.int32

NORM_EPS = 1e-6
LNX_EPS = 64e-5
RW_HEAD = 64
ATT_HEAD_DIM = 128
KV_HEADS = 4
IDX_HEADS = 16
IDX_DIM = 64
PAGE_SIZE = 128
TOPK_MAX = 256
CONV_W = 3
DECAY_LORA = 96
ICLR_LORA = 96
GATE_LORA = 256

LANES = 128
SUBLANES = 8
VMEM_LIMIT = 56 * 1024 * 1024
NEG_INF = float("-inf")
INT_MIN = -2 ** 31


def _cparams(n_axes):
    return pltpu.CompilerParams(dimension_semantics=("arbitrary",) * n_axes,
                                vmem_limit_bytes=VMEM_LIMIT)


def _expand_rows(v, tm):
    s, w = v.shape
    if s == 1:
        return v
    return jnp.broadcast_to(v[:, None, :], (s, tm // s, w)).reshape(tm, w)


def _sigmoid(x):
    return 1.0 / (1.0 + jnp.exp(-x))


def _in_layout(d):
    rw = d
    att = d
    kv = KV_HEADS * ATT_HEAD_DIM
    qi = IDX_HEADS * IDX_DIM
    off = {}
    pos = 0
    for name, width in (("ga", d), ("gb", d), ("r", rw), ("k", rw), ("v", rw), ("q", att), ("qi", qi),
                        ("kk", kv), ("vv", kv), ("lora", 512), ("ki2", LANES), ("wi", LANES)):
        assert pos % width == 0, (name, pos, width)
        off[name] = pos
        pos += width
    total = -(-pos // 1024) * 1024
    return off, total


def _pack_cols(w, d):
    rw, att = d, d
    kv = KV_HEADS * ATT_HEAD_DIM
    qi = IDX_HEADS * IDX_DIM
    sizes = (rw, rw, rw, DECAY_LORA, ICLR_LORA, GATE_LORA, att, kv, kv, qi, IDX_DIM, IDX_HEADS, d, d)
    cuts = np.cumsum((0,) + sizes)
    seg = {n: w[..., cuts[i]:cuts[i + 1]] for i, n in enumerate(
        ("r", "k", "v", "wd", "ad", "gd", "q", "kk", "vv", "qi", "ki", "wi", "ga", "gb"))}
    z = lambda n: jnp.zeros(w.shape[:-1] + (n,), w.dtype)
    _, total = _in_layout(d)
    parts = [seg["ga"], seg["gb"], seg["r"], seg["k"], seg["v"], seg["q"], seg["qi"], seg["kk"], seg["vv"],
             seg["wd"], z(LANES - DECAY_LORA), seg["ad"], z(LANES - ICLR_LORA), seg["gd"],
             seg["ki"], seg["ki"], seg["wi"], z(LANES - IDX_HEADS)]
    used = sum(p.shape[-1] for p in parts)
    parts.append(z(total - used))
    return jnp.concatenate(parts, axis=-1)


def _mm_kernel(*refs, pre_silu, has_bias, has_res):
    x_ref, w_ref = refs[0], refs[1]
    o_ref = refs[-1]
    x = x_ref[...]
    if pre_silu:
        x = x.astype(F32)
        x = x * _sigmoid(x)
    acc = jnp.dot(x.astype(BF16), w_ref[...], preferred_element_type=F32)
    nxt = 2
    if has_bias:
        acc = acc + refs[nxt][...]
        nxt += 1
    if has_res:
        res_ref, gate_ref = refs[nxt], refs[nxt + 1]
        acc = res_ref[...] + _expand_rows(gate_ref[...], acc.shape[0]) * acc
    o_ref[...] = acc.astype(o_ref.dtype)


def _matmul(x, w, *, tm, tn, bias=None, pre_silu=False, res=None, gate=None, gate_col=0, seq_len=None,
            out_dtype=F32, name="mm"):
    m, k = x.shape
    n = w.shape[1]
    tm = min(tm, m)
    assert m % tm == 0 and n % tn == 0, (m, tm, n, tn)
    in_specs = [pl.BlockSpec((tm, k), lambda i, j: (i, 0)),
                pl.BlockSpec((k, tn), lambda i, j: (0, j))]
    args = [x, w]
    if bias is not None:
        in_specs.append(pl.BlockSpec((1, tn), lambda i, j: (0, j)))
        args.append(bias)
    if res is not None:
        in_specs.append(pl.BlockSpec((tm, tn), lambda i, j: (i, j)))
        args.append(res)
        gcol = gate_col * (n // tn)
        if seq_len >= tm:
            in_specs.append(pl.BlockSpec((1, tn), lambda i, j: ((i * tm) // seq_len, gcol + j)))
        else:
            in_specs.append(pl.BlockSpec((tm // seq_len, tn), lambda i, j: (i, gcol + j)))
        args.append(gate)
    return pl.pallas_call(
        functools.partial(_mm_kernel, pre_silu=pre_silu, has_bias=bias is not None, has_res=res is not None),
        grid=(m // tm, n // tn),
        in_specs=in_specs,
        out_specs=pl.BlockSpec((tm, tn), lambda i, j: (i, j)),
        out_shape=jax.ShapeDtypeStruct((m, n), out_dtype),
        compiler_params=_cparams(2),
        name=name,
    )(*args)


def _norm_mod_kernel(x_ref, g_ref, sc_ref, sh_ref, o_ref):
    x = x_ref[...]
    tm = x.shape[0]
    y = x * lax.rsqrt(jnp.mean(x * x, axis=-1, keepdims=True) + NORM_EPS)
    y = y * g_ref[...]
    y = y * (1.0 + _expand_rows(sc_ref[...], tm)) + _expand_rows(sh_ref[...], tm)
    o_ref[...] = y.astype(o_ref.dtype)


def _mod_spec(tm, width, col, seq_len):
    if seq_len >= tm:
        return pl.BlockSpec((1, width), lambda i: ((i * tm) // seq_len, col))
    return pl.BlockSpec((tm // seq_len, width), lambda i: (i, col))


def _norm_mod(x, g, mod, sc_col, sh_col, seq_len, tm=256):
    m, d = x.shape
    tm = min(tm, m)
    return pl.pallas_call(
        _norm_mod_kernel,
        grid=(m // tm,),
        in_specs=[pl.BlockSpec((tm, d), lambda i: (i, 0)),
                  pl.BlockSpec((1, d), lambda i: (0, 0)),
                  _mod_spec(tm, d, sc_col, seq_len),
                  _mod_spec(tm, d, sh_col, seq_len)],
        out_specs=pl.BlockSpec((tm, d), lambda i: (i, 0)),
        out_shape=jax.ShapeDtypeStruct((m, d), BF16),
        compiler_params=_cparams(1),
        name="norm_mod",
    )(x, g, mod, mod)


def _prev_rows(x, halo_ref, state_refs, i, seq_len, shift):
    tm = x.shape[0]
    row = lax.broadcasted_iota(I32, (tm, 1), 0)
    out = pltpu.roll(x, shift, 0)
    n_state = len(state_refs)
    for r in range(shift):
        back = shift - r
        state_row = state_refs[n_state - back][...]
        if seq_len >= tm:
            src = jnp.where(i == 0, state_row, halo_ref[SUBLANES - back:SUBLANES - back + 1, :])
            out = jnp.where(row == r, src, out)
        else:
            out = jnp.where(row % seq_len == r, _expand_rows(state_row, tm), out)
    return out


def _halo_spec(tm, width, col):
    return pl.BlockSpec((SUBLANES, width), lambda i: (jnp.maximum(i * (tm // SUBLANES) - 1, 0), col))


def _state_spec(tm, width, col, seq_len):
    if seq_len >= tm:
        return pl.BlockSpec((1, width), lambda i: ((i * tm) // seq_len, col))
    return pl.BlockSpec((tm // seq_len, width), lambda i: (i, col))


def _softplus(x):
    return jnp.maximum(x, 0.0) + jnp.log1p(jnp.exp(-jnp.abs(x)))


def _rwkv_prep_kernel(zr_ref, zk_ref, zv_ref, zl_ref, hr_ref, hk_ref, hv_ref, hl_ref,
                      sr_ref, sk_ref, sv_ref, sl_ref, mur_ref, muk_ref, muv_ref, mul_ref,
                      w0_ref, wdec_ref, a0_ref, wa_ref, wg_ref,
                      r_ref, k_ref, v_ref, dec_ref, ic_ref, g_ref, *, seq_len):
    i = pl.program_id(0)

    def shifted(z_ref, h_ref, s_ref, mu_ref):
        x = z_ref[...]
        prev = _prev_rows(x, h_ref, [s_ref], i, seq_len, 1)
        return x + (prev - x) * mu_ref[...]

    r_ref[...] = shifted(zr_ref, hr_ref, sr_ref, mur_ref)
    k_ref[...] = shifted(zk_ref, hk_ref, sk_ref, muk_ref)
    v_ref[...] = shifted(zv_ref, hv_ref, sv_ref, muv_ref)
    zl = shifted(zl_ref, hl_ref, sl_ref, mul_ref)
    wd = zl[:, 0:LANES]
    ad = zl[:, LANES:2 * LANES]
    gd = zl[:, 2 * LANES:4 * LANES]
    mm = lambda a, w_ref: jnp.dot(a.astype(BF16), w_ref[...], preferred_element_type=F32)
    dec_in = w0_ref[...] + mm(jnp.tanh(wd), wdec_ref)
    w_log = -_softplus(-dec_in) - 0.5
    dec_ref[...] = jnp.exp(-jnp.exp(w_log))
    ic_ref[...] = _sigmoid(a0_ref[...] + mm(ad, wa_ref))
    g_ref[...] = mm(_sigmoid(gd), wg_ref)


def _rwkv_prep(z, off, state_packed, mu_packed, w0, wdec, a0, wa, wg, seq_len, d, tm=256):
    m = z.shape[0]
    tm = min(tm, m)
    cb = lambda name, width: off[name] // width
    zspec = lambda name, width: pl.BlockSpec((tm, width), lambda i: (i, cb(name, width)))
    sspec = lambda c, width: _state_spec(tm, width, c, seq_len)
    mspec = lambda c, width: pl.BlockSpec((1, width), lambda i: (0, c))
    full = lambda a: pl.BlockSpec(a.shape, lambda i: (0,) * a.ndim)
    lcol = 3 * d // 512
    out_spec = pl.BlockSpec((tm, d), lambda i: (i, 0))
    out_sds = jax.ShapeDtypeStruct((m, d), F32)
    return pl.pallas_call(
        functools.partial(_rwkv_prep_kernel, seq_len=seq_len),
        grid=(m // tm,),
        in_specs=[zspec("r", d), zspec("k", d), zspec("v", d), zspec("lora", 512),
                  _halo_spec(tm, d, cb("r", d)), _halo_spec(tm, d, cb("k", d)), _halo_spec(tm, d, cb("v", d)),
                  _halo_spec(tm, 512, cb("lora", 512)),
                  sspec(0, d), sspec(1, d), sspec(2, d), sspec(lcol, 512),
                  mspec(0, d), mspec(1, d), mspec(2, d), mspec(lcol, 512),
                  full(w0), full(wdec), full(a0), full(wa), full(wg)],
        out_specs=[out_spec] * 6,
        out_shape=[out_sds] * 6,
        compiler_params=_cparams(1),
        name="rwkv_prep",
    )(z, z, z, z, z, z, z, z, state_packed, state_packed, state_packed, state_packed,
      mu_packed, mu_packed, mu_packed, mu_packed, w0, wdec, a0, wa, wg)


def _scan_kernel(r_ref, w_ref, k_ref, v_ref, ic_ref, g_ref, s0_ref, kk_ref, ka_ref, rk_ref, lw_ref, lb_ref,
                 o_ref, sout_ref, s_scr, a_scr, b_scr, km_scr, o_scr, *, gsub):
    ti = pl.program_id(1)
    gb, tb, n = r_ref.shape

    @pl.when(ti == 0)
    def _():
        s_scr[...] = s0_ref[...]

    for g in range(gb):
        kraw = k_ref[g]
        ic = ic_ref[g]
        kk = kraw * kk_ref[g]
        nrm = jnp.sqrt(jnp.sum(kk * kk, axis=-1, keepdims=True))
        kk = kk / jnp.maximum(nrm, 1e-12)
        a_scr[g] = -kk
        b_scr[g] = kk * ic
        km_scr[g] = kraw * (1.0 + (ic - 1.0) * ka_ref[g])

    eye = (lax.broadcasted_iota(I32, (n, n), 0) == lax.broadcasted_iota(I32, (n, n), 1)).astype(F32)

    for g0 in range(0, gb, gsub):
        heads = range(g0, min(g0 + gsub, gb))

        def step(t, carry):
            new = []
            for s, g in zip(carry, heads):
                row = lambda ref: ref[g, pl.ds(t, 1), :]
                sa = jnp.sum(s * row(a_scr), axis=1, keepdims=True)
                vcol = jnp.sum(eye * row(v_ref), axis=1, keepdims=True)
                s = s * row(w_ref) + sa * row(b_scr) + vcol * row(km_scr)
                ocol = jnp.sum(s * row(r_ref), axis=1, keepdims=True)
                o_scr[g, pl.ds(t, 1), :] = jnp.sum(eye * ocol, axis=0, keepdims=True)
                new.append(s)
            return tuple(new)

        fin = lax.fori_loop(0, tb, step, tuple(s_scr[g] for g in heads))
        for s, g in zip(fin, heads):
            s_scr[g] = s

    for g in range(gb):
        o = o_scr[g]
        mu = jnp.mean(o, axis=-1, keepdims=True)
        var = jnp.mean(jnp.square(o - mu), axis=-1, keepdims=True)
        on = (o - mu) * lax.rsqrt(var + LNX_EPS) * lw_ref[g] + lb_ref[g]
        bonus = jnp.sum(r_ref[g] * km_scr[g] * rk_ref[g], axis=-1, keepdims=True) * v_ref[g]
        o_ref[g] = (on + bonus) * g_ref[g]

    sout_ref[...] = s_scr[...]


def _rwkv_scan(seqs, s0, params, gb, tb, gsub=4):
    g_tot, t_tot, n = seqs[0].shape
    n_heads = params[0].shape[0]
    assert g_tot % gb == 0 and t_tot % tb == 0 and n_heads % gb == 0
    hblocks = n_heads // gb
    seq_spec = pl.BlockSpec((gb, tb, n), lambda gi, ti: (gi, ti, 0))
    st_spec = pl.BlockSpec((gb, n, n), lambda gi, ti: (gi, 0, 0))
    par_spec = pl.BlockSpec((gb, 1, n), lambda gi, ti: (gi % hblocks, 0, 0))
    return pl.pallas_call(
        functools.partial(_scan_kernel, gsub=gsub),
        grid=(g_tot // gb, t_tot // tb),
        in_specs=[seq_spec] * 6 + [st_spec] + [par_spec] * 5,
        out_specs=[seq_spec, st_spec],
        out_shape=[jax.ShapeDtypeStruct((g_tot, t_tot, n), F32), jax.ShapeDtypeStruct((g_tot, n, n), F32)],
        scratch_shapes=[pltpu.VMEM((gb, n, n), F32)] + [pltpu.VMEM((gb, tb, n), F32)] * 4,
        compiler_params=_cparams(2),
        name="rwkv_scan",
    )(*seqs, s0, *params)


def _dsa_prep_kernel(q_ref, k_ref, qg_ref, kg_ref, qo_ref, ko_ref, kb_ref):
    def head_norm(x, g):
        outs = []
        for h in range(x.shape[1] // ATT_HEAD_DIM):
            xh = x[:, h * ATT_HEAD_DIM:(h + 1) * ATT_HEAD_DIM]
            outs.append(xh * lax.rsqrt(jnp.mean(xh * xh, axis=-1, keepdims=True) + NORM_EPS) * g)
        return jnp.concatenate(outs, axis=1)

    qo_ref[...] = head_norm(q_ref[...], qg_ref[...]).astype(qo_ref.dtype)
    kn = head_norm(k_ref[...], kg_ref[...])
    ko_ref[...] = kn
    kb_ref[...] = kn.astype(kb_ref.dtype)


def _dsa_prep(z, off, qg, kg, d, tm=256):
    m = z.shape[0]
    tm = min(tm, m)
    kv = KV_HEADS * ATT_HEAD_DIM
    return pl.pallas_call(
        _dsa_prep_kernel,
        grid=(m // tm,),
        in_specs=[pl.BlockSpec((tm, d), lambda i: (i, off["q"] // d)),
                  pl.BlockSpec((tm, kv), lambda i: (i, off["kk"] // kv)),
                  pl.BlockSpec((1, ATT_HEAD_DIM), lambda i: (0, 0)),
                  pl.BlockSpec((1, ATT_HEAD_DIM), lambda i: (0, 0))],
        out_specs=[pl.BlockSpec((tm, d), lambda i: (i, 0)),
                   pl.BlockSpec((tm, kv), lambda i: (i, 0)),
                   pl.BlockSpec((tm, kv), lambda i: (i, 0))],
        out_shape=[jax.ShapeDtypeStruct((m, d), BF16), jax.ShapeDtypeStruct((m, kv), F32),
                   jax.ShapeDtypeStruct((m, kv), BF16)],
        compiler_params=_cparams(1),
        name="dsa_prep",
    )(z, z, qg, kg)


def _sort_key(x):
    x = jnp.where(x == 0.0, 0.0, x)
    bits = pltpu.bitcast(x, I32)
    return bits ^ (lax.shift_right_arithmetic(bits, 31) & 0x7FFFFFFF)


def _topk_threshold(skey_ref, n_blocks, k_top, rows):
    def count(pred_fn):
        def body(j, acc):
            return acc + jnp.where(pred_fn(skey_ref[j]), 1.0, 0.0)
        acc = lax.fori_loop(0, n_blocks, body, jnp.zeros((rows, LANES), F32))
        return jnp.sum(acc, axis=1, keepdims=True)

    def bit_body(it, tau):
        cand = tau + lax.shift_left(jnp.int32(1), 31 - it)
        return jnp.where(count(lambda s: s >= cand) >= k_top, cand, tau)

    tau = lax.fori_loop(0, 32, bit_body, jnp.full((rows, 1), INT_MIN, I32))
    need = k_top - count(lambda s: s > tau)
    return tau, need


def _selected(skey, tau, need, eq_before, tri):
    eq = skey == tau
    eqf = jnp.where(eq, 1.0, 0.0)
    prefix = eq_before + jnp.dot(eqf.astype(BF16), tri, preferred_element_type=F32)
    sel = (skey > tau) | (eq & (prefix <= need))
    return sel, eq_before + jnp.sum(eqf, axis=1, keepdims=True)


def _tri_incl():
    return (lax.broadcasted_iota(I32, (LANES, LANES), 0) <= lax.broadcasted_iota(I32, (LANES, LANES), 1)).astype(BF16)


def _sel_prompt_kernel(qi_ref, wi_ref, ki_ref, bias_ref, skey_scr, qm_scr, wb_scr, *, k_top):
    i = pl.program_id(0)
    qb = qi_ref.shape[0]
    nkb = bias_ref.shape[1]
    lane = lax.broadcasted_iota(I32, (qb, LANES), 1)
    rowq = lax.broadcasted_iota(I32, (qb, LANES), 0)
    wi = wi_ref[...] * (IDX_HEADS ** -0.5)
    for hp in range(IDX_HEADS // 2):
        qpair = qi_ref[:, hp * LANES:(hp + 1) * LANES]
        qm_scr[2 * hp] = jnp.where(lane < IDX_DIM, qpair, 0.0).astype(BF16)
        qm_scr[2 * hp + 1] = jnp.where(lane >= IDX_DIM, qpair, 0.0).astype(BF16)
    for h in range(IDX_HEADS):
        wb_scr[h] = jnp.broadcast_to(wi[:, h:h + 1], (qb, LANES))

    def score_block(j, carry):
        kblk = ki_ref[pl.ds(pl.multiple_of(j * LANES, LANES), LANES), :].astype(BF16)
        sc = jnp.zeros((qb, LANES), F32)
        for h in range(IDX_HEADS):
            dots = lax.dot_general(qm_scr[h], kblk, (((1,), (1,)), ((), ())), preferred_element_type=F32)
            sc = sc + jnp.maximum(dots * (IDX_DIM ** -0.5), 0.0) * wb_scr[h]
        allowed = (j < i) | (lane <= rowq)
        skey_scr[j] = _sort_key(jnp.where(allowed, sc, NEG_INF))
        return carry

    lax.fori_loop(0, i + 1, score_block, 0)
    tau, need = _topk_threshold(skey_scr, i + 1, float(k_top), qb)
    tri = _tri_incl()

    def write_block(j, eq_before):
        sel, eq_after = _selected(skey_scr[j], tau, need, eq_before, tri)
        allowed = (j < i) | (lane <= rowq)
        bias_ref[0, j] = jnp.where(sel & allowed, 0.0, NEG_INF).astype(bias_ref.dtype)
        return eq_after

    lax.fori_loop(0, i + 1, write_block, jnp.zeros((qb, 1), F32))

    def fill_block(j, carry):
        bias_ref[0, j] = jnp.full((qb, LANES), NEG_INF, bias_ref.dtype)
        return carry

    lax.fori_loop(i + 1, nkb, fill_block, 0)


def _sel_prompt(z, off, k_top):
    t = z.shape[0]
    qb = LANES
    nkb = t // LANES
    nqb = t // qb
    qi_w = IDX_HEADS * IDX_DIM
    return pl.pallas_call(
        functools.partial(_sel_prompt_kernel, k_top=k_top),
        grid=(nqb,),
        in_specs=[pl.BlockSpec((qb, qi_w), lambda i: (i, off["qi"] // qi_w)),
                  pl.BlockSpec((qb, LANES), lambda i: (i, off["wi"] // LANES)),
                  pl.BlockSpec((t, LANES), lambda i: (0, off["ki2"] // LANES))],
        out_specs=pl.BlockSpec((1, nkb, qb, LANES), lambda i: (i, 0, 0, 0)),
        out_shape=jax.ShapeDtypeStruct((nqb, nkb, qb, LANES), BF16),
        scratch_shapes=[pltpu.VMEM((nkb, qb, LANES), I32),
                        pltpu.VMEM((IDX_HEADS, qb, LANES), BF16),
                        pltpu.VMEM((IDX_HEADS, qb, LANES), F32)],
        compiler_params=_cparams(1),
        name="dsa_select_prompt",
    )(z, z, z)


def _attn_step(q_heads, k, v, bias, m_scr, l_scr, acc_scr, first):
    for n, q in enumerate(q_heads):
        kn = k[:, n * ATT_HEAD_DIM:(n + 1) * ATT_HEAD_DIM]
        vn = v[:, n * ATT_HEAD_DIM:(n + 1) * ATT_HEAD_DIM]
        s = lax.dot_general(q, kn, (((1,), (1,)), ((), ())), preferred_element_type=F32)
        s = s * (ATT_HEAD_DIM ** -0.5) + bias[n]
        m_prev = jnp.where(first, NEG_INF, m_scr[n])
        l_prev = jnp.where(first, 0.0, l_scr[n])
        acc_prev = jnp.where(first, 0.0, acc_scr[n])
        m_cur = jnp.maximum(m_prev, jnp.max(s, axis=1, keepdims=True))
        m_safe = jnp.where(m_cur == NEG_INF, 0.0, m_cur)
        p = jnp.exp(s - m_safe)
        alpha = jnp.exp(m_prev - m_safe)
        l_scr[n] = alpha * l_prev + jnp.sum(p, axis=1, keepdims=True)
        acc_scr[n] = alpha * acc_prev + jnp.dot(p.astype(BF16), vn, preferred_element_type=F32)
        m_scr[n] = m_cur


def _att_prompt_kernel(qidx_ref, kidx_ref, q_ref, k_ref, v_ref, bias_ref, o_ref, m_scr, l_scr, acc_scr):
    p = pl.program_id(0)
    qi = qidx_ref[p]
    kj = kidx_ref[p]
    n_heads = q_ref.shape[1] // ATT_HEAD_DIM
    group = n_heads // KV_HEADS
    bias = bias_ref[0, 0].astype(F32)
    q = q_ref[...]
    k = k_ref[...]
    v = v_ref[...]
    for h in range(n_heads):
        n = h // group
        _attn_step([q[:, h * ATT_HEAD_DIM:(h + 1) * ATT_HEAD_DIM]],
                   k[:, n * ATT_HEAD_DIM:(n + 1) * ATT_HEAD_DIM], v[:, n * ATT_HEAD_DIM:(n + 1) * ATT_HEAD_DIM],
                   [bias], m_scr.at[h:h + 1], l_scr.at[h:h + 1], acc_scr.at[h:h + 1], kj == 0)

    @pl.when(kj == qi)
    def _():
        for h in range(n_heads):
            o_ref[:, h * ATT_HEAD_DIM:(h + 1) * ATT_HEAD_DIM] = (acc_scr[h] / l_scr[h]).astype(o_ref.dtype)


def _att_prompt(q, kb, vb, bias):
    t, d = q.shape
    qb = LANES
    kv = kb.shape[1]
    nqb = t // qb
    pairs = [(i, j) for i in range(nqb) for j in range(i + 1)]
    qidx = jnp.asarray(np.array([p[0] for p in pairs], np.int32))
    kidx = jnp.asarray(np.array([p[1] for p in pairs], np.int32))
    n_heads = d // ATT_HEAD_DIM
    grid_spec = pltpu.PrefetchScalarGridSpec(
        num_scalar_prefetch=2,
        grid=(len(pairs),),
        in_specs=[pl.BlockSpec((qb, d), lambda p, qi, kj: (qi[p], 0)),
                  pl.BlockSpec((LANES, kv), lambda p, qi, kj: (kj[p], 0)),
                  pl.BlockSpec((LANES, kv), lambda p, qi, kj: (kj[p], 0)),
                  pl.BlockSpec((1, 1, qb, LANES), lambda p, qi, kj: (qi[p], kj[p], 0, 0))],
        out_specs=pl.BlockSpec((qb, d), lambda p, qi, kj: (qi[p], 0)),
        scratch_shapes=[pltpu.VMEM((n_heads, qb, 1), F32), pltpu.VMEM((n_heads, qb, 1), F32),
                        pltpu.VMEM((n_heads, qb, ATT_HEAD_DIM), F32)],
    )
    return pl.pallas_call(
        _att_prompt_kernel,
        grid_spec=grid_spec,
        out_shape=jax.ShapeDtypeStruct((t, d), BF16),
        compiler_params=_cparams(1),
        name="dsa_attn_prompt",
    )(qidx, kidx, q, kb, vb, bias)


def _sel_sample_kernel(pt_ref, qall_ref, wcol_ref, kpage_ref, knew_ref, bias_ref, skey_scr, knew_scr, *, k_top):
    j = pl.program_id(1)
    n_pages = pl.num_programs(1) - 1
    t_new = knew_ref.shape[1]
    rows = qall_ref.shape[1]
    lane = lax.broadcasted_iota(I32, (t_new, LANES), 1)
    rowq = lax.broadcasted_iota(I32, (t_new, LANES), 0)

    knew_scr[...] = jnp.zeros_like(knew_scr)
    knew_scr[0:t_new, :] = knew_ref[0]
    kblk = jnp.where(j < n_pages, kpage_ref[0], knew_scr[...]).astype(BF16)
    dots = lax.dot_general(qall_ref[0].astype(BF16), kblk, (((1,), (1,)), ((), ())), preferred_element_type=F32)
    part = jnp.maximum(dots * (IDX_DIM ** -0.5), 0.0) * (wcol_ref[0] * (IDX_HEADS ** -0.5))
    sc = part[0:t_new]
    for h in range(1, rows // t_new):
        sc = sc + part[h * t_new:(h + 1) * t_new]
    allowed = (j < n_pages) | (lane <= rowq)
    skey_scr[j] = _sort_key(jnp.where(allowed, sc, NEG_INF))

    @pl.when(j == n_pages)
    def _():
        tau, need = _topk_threshold(skey_scr, n_pages + 1, float(k_top), t_new)
        tri = _tri_incl()
        eq_before = jnp.zeros((t_new, 1), F32)
        for jj in range(skey_scr.shape[0]):
            sel, eq_before = _selected(skey_scr[jj], tau, need, eq_before, tri)
            ok = sel if jj < skey_scr.shape[0] - 1 else sel & (lane <= rowq)
            bias_ref[0, jj] = jnp.where(ok, 0.0, NEG_INF)


def _sel_sample(page_table, qall, wcol, cache_kidx, knew, k_top):
    b, n_pages = page_table.shape
    t_new = knew.shape[1]
    rows = qall.shape[1]
    grid_spec = pltpu.PrefetchScalarGridSpec(
        num_scalar_prefetch=1,
        grid=(b, n_pages + 1),
        in_specs=[pl.BlockSpec((1, rows, IDX_DIM), lambda bi, j, pt: (bi, 0, 0)),
                  pl.BlockSpec((1, rows, 1), lambda bi, j, pt: (bi, 0, 0)),
                  pl.BlockSpec((1, PAGE_SIZE, IDX_DIM),
                               lambda bi, j, pt: (pt[bi, jnp.minimum(j, n_pages - 1)], 0, 0)),
                  pl.BlockSpec((1, t_new, IDX_DIM), lambda bi, j, pt: (bi, 0, 0))],
        out_specs=pl.BlockSpec((1, n_pages + 1, t_new, LANES), lambda bi, j, pt: (bi, 0, 0, 0)),
        scratch_shapes=[pltpu.VMEM((n_pages + 1, t_new, LANES), I32), pltpu.VMEM((PAGE_SIZE, IDX_DIM), F32)],
    )
    return pl.pallas_call(
        functools.partial(_sel_sample_kernel, k_top=k_top),
        grid_spec=grid_spec,
        out_shape=jax.ShapeDtypeStruct((b, n_pages + 1, t_new, LANES), F32),
        compiler_params=_cparams(2),
        name="dsa_select_sample",
    )(page_table, qall, wcol, cache_kidx, knew)


def _att_sample_kernel(pt_ref, q_ref, kpage_ref, vpage_ref, knew_ref, vnew_ref, bias_ref, o_ref,
                       m_scr, l_scr, acc_scr, knew_scr, vnew_scr):
    j = pl.program_id(1)
    n_pages = pl.num_programs(1) - 1
    t_new = knew_ref.shape[1]
    group_rows = q_ref.shape[2]
    knew_scr[...] = jnp.zeros_like(knew_scr)
    vnew_scr[...] = jnp.zeros_like(vnew_scr)
    knew_scr[0:t_new, :] = knew_ref[0]
    vnew_scr[0:t_new, :] = vnew_ref[0]
    is_page = j < n_pages
    k = jnp.where(is_page, kpage_ref[0], knew_scr[...]).astype(BF16)
    v = jnp.where(is_page, vpage_ref[0], vnew_scr[...]).astype(BF16)
    b8 = bias_ref[0, 0]
    bias = jnp.concatenate([b8] * (group_rows // t_new), axis=0)
    _attn_step([q_ref[0, n] for n in range(KV_HEADS)], k, v, [bias] * KV_HEADS, m_scr, l_scr, acc_scr, j == 0)

    @pl.when(j == n_pages)
    def _():
        for n in range(KV_HEADS):
            o_ref[0, n] = acc_scr[n] / l_scr[n]


def _att_sample(page_table, qg, cache_k, cache_v, knew, vnew, bias):
    b, n_pages = page_table.shape
    t_new = knew.shape[1]
    kv = knew.shape[2]
    group_rows = qg.shape[2]
    page_map = lambda bi, j, pt: (pt[bi, jnp.minimum(j, n_pages - 1)], 0, 0)
    grid_spec = pltpu.PrefetchScalarGridSpec(
        num_scalar_prefetch=1,
        grid=(b, n_pages + 1),
        in_specs=[pl.BlockSpec((1, KV_HEADS, group_rows, ATT_HEAD_DIM), lambda bi, j, pt: (bi, 0, 0, 0)),
                  pl.BlockSpec((1, PAGE_SIZE, kv), page_map),
                  pl.BlockSpec((1, PAGE_SIZE, kv), page_map),
                  pl.BlockSpec((1, t_new, kv), lambda bi, j, pt: (bi, 0, 0)),
                  pl.BlockSpec((1, t_new, kv), lambda bi, j, pt: (bi, 0, 0)),
                  pl.BlockSpec((1, 1, t_new, LANES), lambda bi, j, pt: (bi, j, 0, 0))],
        out_specs=pl.BlockSpec((1, KV_HEADS, group_rows, ATT_HEAD_DIM), lambda bi, j, pt: (bi, 0, 0, 0)),
        scratch_shapes=[pltpu.VMEM((KV_HEADS, group_rows, 1), F32), pltpu.VMEM((KV_HEADS, group_rows, 1), F32),
                        pltpu.VMEM((KV_HEADS, group_rows, ATT_HEAD_DIM), F32),
                        pltpu.VMEM((PAGE_SIZE, kv), F32), pltpu.VMEM((PAGE_SIZE, kv), F32)],
    )
    return pl.pallas_call(
        _att_sample_kernel,
        grid_spec=grid_spec,
        out_shape=jax.ShapeDtypeStruct((b, KV_HEADS, group_rows, ATT_HEAD_DIM), F32),
        compiler_params=_cparams(2),
        name="dsa_attn_sample",
    )(page_table, qg, cache_k, cache_v, knew, vnew, bias)


def _mix_kernel(xa_ref, xb_ref, wa_ref, wb_ref, ga_ref, gb_ref, o_ref):
    pa = jnp.dot(xa_ref[...], wa_ref[...], preferred_element_type=F32)
    pb = jnp.dot(xb_ref[...], wb_ref[...], preferred_element_type=F32)
    o_ref[...] = (_sigmoid(ga_ref[...]) * pa + _sigmoid(gb_ref[...]) * pb).astype(o_ref.dtype)


def _mix(o_rw, o_att, wa, wb, z, off, d, tm=512, tn=512):
    m = o_rw.shape[0]
    tm = min(tm, m)
    nb = d // tn
    return pl.pallas_call(
        _mix_kernel,
        grid=(m // tm, nb),
        in_specs=[pl.BlockSpec((tm, d), lambda i, j: (i, 0)),
                  pl.BlockSpec((tm, d), lambda i, j: (i, 0)),
                  pl.BlockSpec((d, tn), lambda i, j: (0, j)),
                  pl.BlockSpec((d, tn), lambda i, j: (0, j)),
                  pl.BlockSpec((tm, tn), lambda i, j: (i, off["ga"] // tn + j)),
                  pl.BlockSpec((tm, tn), lambda i, j: (i, off["gb"] // tn + j))],
        out_specs=pl.BlockSpec((tm, tn), lambda i, j: (i, j)),
        out_shape=jax.ShapeDtypeStruct((m, d), BF16),
        compiler_params=_cparams(2),
        name="branch_mix",
    )(o_rw, o_att, wa, wb, z, z)


def _conv_kernel(ug_ref, uv_ref, halo_ref, st0_ref, st1_ref, cw_ref, cb_ref, o_ref, *, seq_len):
    i = pl.program_id(0)
    x = ug_ref[...]
    acc = cb_ref[...] + x * cw_ref[CONV_W - 1:CONV_W, :]
    for back in range(1, CONV_W):
        prev = _prev_rows(x, halo_ref, [st0_ref, st1_ref], i, seq_len, back)
        acc = acc + prev * cw_ref[CONV_W - 1 - back:CONV_W - back, :]
    o_ref[...] = (acc * _sigmoid(acc) * uv_ref[...]).astype(o_ref.dtype)


def _conv_ffn(u, conv_prev, conv_w, conv_b, seq_len, d_ff, tm=256, tn=512):
    m = u.shape[0]
    tm = min(tm, m)
    nb = d_ff // tn
    assert conv_prev.shape[1] == CONV_W - 1 == 2
    if seq_len >= tm:
        st_spec = pl.BlockSpec((1, tn), lambda i, j: ((i * tm) // seq_len, j))
    else:
        st_spec = pl.BlockSpec((tm // seq_len, tn), lambda i, j: (i, j))
    return pl.pallas_call(
        functools.partial(_conv_kernel, seq_len=seq_len),
        grid=(m // tm, nb),
        in_specs=[pl.BlockSpec((tm, tn), lambda i, j: (i, j)),
                  pl.BlockSpec((tm, tn), lambda i, j: (i, nb + j)),
                  pl.BlockSpec((SUBLANES, tn), lambda i, j: (jnp.maximum(i * (tm // SUBLANES) - 1, 0), j)),
                  st_spec, st_spec,
                  pl.BlockSpec((CONV_W, tn), lambda i, j: (0, j)),
                  pl.BlockSpec((1, tn), lambda i, j: (0, j))],
        out_specs=pl.BlockSpec((tm, tn), lambda i, j: (i, j)),
        out_shape=jax.ShapeDtypeStruct((m, d_ff), BF16),
        compiler_params=_cparams(2),
        name="conv_ffn",
    )(u, u, u, conv_prev[:, 0], conv_prev[:, 1], conv_w, conv_b)


def _head_major(x, n_seq, seq_len, n_heads):
    return x.reshape(n_seq, seq_len, n_heads, RW_HEAD).transpose(0, 2, 1, 3).reshape(n_seq * n_heads, seq_len, RW_HEAD)


def _layer(x, mod, shift_prev, s0, conv_prev, W, cache):
    n_seq, seq_len, d = x.shape
    m = n_seq * seq_len
    off = W["off"]
    n_rw = d // RW_HEAD
    kv = KV_HEADS * ATT_HEAD_DIM
    xf = x.reshape(m, d)

    h = _norm_mod(xf, W["norm1_g"], mod, 1, 0, seq_len)
    z = _matmul(h, W["w_in"], tm=1024, tn=1024, name="in_proj")

    lora = lambda a: jnp.concatenate(
        [a[..., 3 * d:3 * d + DECAY_LORA], jnp.zeros(a.shape[:-1] + (LANES - DECAY_LORA,), a.dtype),
         a[..., 3 * d + DECAY_LORA:3 * d + DECAY_LORA + ICLR_LORA],
         jnp.zeros(a.shape[:-1] + (LANES - ICLR_LORA,), a.dtype),
         a[..., 3 * d + DECAY_LORA + ICLR_LORA:]], axis=-1)
    pack_rw = lambda a: jnp.concatenate([a[..., :3 * d], lora(a)], axis=-1)
    state_packed = pack_rw(shift_prev)
    r, k_raw, v_rw, decay, iclr, gate = _rwkv_prep(
        z, off, state_packed, W["mu_packed"], W["w0"], W["w_decay_up"], W["a0"], W["w_a_up"], W["w_g_up"],
        seq_len, d)
    seqs = [_head_major(a, n_seq, seq_len, n_rw) for a in (r, decay, k_raw, v_rw, iclr, gate)]
    if seq_len >= 512:
        gb, tb = 4, 512
    else:
        gb, tb = n_rw, seq_len
    o_heads, s_new = _rwkv_scan(seqs, s0.reshape(n_seq * n_rw, RW_HEAD, RW_HEAD), W["rw_params"], gb, tb)
    o_rw = o_heads.reshape(n_seq, n_rw, seq_len, RW_HEAD).transpose(0, 2, 1, 3).reshape(m, d).astype(BF16)
    s_new = s_new.reshape(n_seq, n_rw, RW_HEAD, RW_HEAD)
    lora_cols = z[:, off["lora"]:off["lora"] + 512].reshape(n_seq, seq_len, 512)[:, -1]
    shift_new = jnp.concatenate(
        [z[:, off["r"]:off["r"] + 3 * d].reshape(n_seq, seq_len, 3 * d)[:, -1],
         lora_cols[:, :DECAY_LORA], lora_cols[:, LANES:LANES + ICLR_LORA], lora_cols[:, 2 * LANES:]], axis=-1)

    qn, k_new, k_bf = _dsa_prep(z, off, W["q_norm_g"], W["k_norm_g"], d)
    v_new = z[:, off["vv"]:off["vv"] + kv]
    ki_new = z[:, off["ki2"]:off["ki2"] + IDX_DIM]
    if cache is None:
        assert n_seq == 1
        k_top = min(TOPK_MAX, seq_len // 4)
        bias = _sel_prompt(z, off, k_top)
        o_att = _att_prompt(qn, k_bf, v_new.astype(BF16), bias)
    else:
        cache_k, cache_v, cache_kidx, page_table = cache
        n_pages = page_table.shape[1]
        k_top = min(TOPK_MAX, (n_pages * PAGE_SIZE + seq_len) // 4)
        n_pool = cache_k.shape[0]
        qi = z[:, off["qi"]:off["qi"] + IDX_HEADS * IDX_DIM]
        qall = qi.reshape(n_seq, seq_len, IDX_HEADS, IDX_DIM).transpose(0, 2, 1, 3).reshape(
            n_seq, IDX_HEADS * seq_len, IDX_DIM)
        wi = z[:, off["wi"]:off["wi"] + IDX_HEADS]
        wcol = wi.reshape(n_seq, seq_len, IDX_HEADS).transpose(0, 2, 1).reshape(n_seq, IDX_HEADS * seq_len, 1)
        bias = _sel_sample(page_table, qall, wcol, cache_kidx, ki_new.reshape(n_seq, seq_len, IDX_DIM), k_top)
        group = d // ATT_HEAD_DIM // KV_HEADS
        qg = qn.reshape(n_seq, seq_len, KV_HEADS, group, ATT_HEAD_DIM).transpose(0, 2, 3, 1, 4).reshape(
            n_seq, KV_HEADS, group * seq_len, ATT_HEAD_DIM)
        o_g = _att_sample(page_table, qg, cache_k.reshape(n_pool, PAGE_SIZE, kv), cache_v.reshape(n_pool, PAGE_SIZE, kv),
                          k_new.reshape(n_seq, seq_len, kv), v_new.reshape(n_seq, seq_len, kv), bias)
        o_att = o_g.reshape(n_seq, KV_HEADS, group, seq_len, ATT_HEAD_DIM).transpose(0, 3, 1, 2, 4).reshape(
            m, d).astype(BF16)

    mixed = _mix(o_rw, o_att, W["w_proj_a"], W["w_proj_b"], z, off, d)
    x1 = _matmul(mixed, W["w_out"], tm=512, tn=512, res=xf, gate=mod, gate_col=2, seq_len=seq_len, name="out_proj")

    h2 = _norm_mod(x1, W["norm2_g"], mod, 4, 3, seq_len)
    u = _matmul(h2, W["w_ffn_up"], tm=1024, tn=1024, name="ffn_up")
    d_ff = W["w_ffn_down"].shape[0]
    act = _conv_ffn(u, conv_prev, W["conv_w"], W["conv_b"], seq_len, d_ff)
    x2 = _matmul(act, W["w_ffn_down"], tm=512, tn=512, res=x1, gate=mod, gate_col=5, seq_len=seq_len, name="ffn_down")
    ug = u[:, :d_ff].reshape(n_seq, seq_len, d_ff)
    conv_new = jnp.concatenate([conv_prev.astype(u.dtype), ug], axis=1)[:, -(CONV_W - 1):]

    hd = lambda a, w: a.reshape(n_seq, seq_len, KV_HEADS, w)
    return (x2.reshape(n_seq, seq_len, d), hd(k_new, ATT_HEAD_DIM), hd(v_new, ATT_HEAD_DIM),
            ki_new.reshape(n_seq, seq_len, IDX_DIM), s_new, shift_new, conv_new)


def kernel(x_prompt, x_sample, c_prompt, c_sample, cache_k, cache_v, cache_kidx, page_table, state_rwkv,
           state_shift, state_conv, w_ada, b_ada, norm1_g, norm2_g, w_in, shift_mu, w0, w_decay_up, a0, w_a_up,
           w_g_up, k_k, k_a, r_k, lnx_w, lnx_b, q_norm_g, k_norm_g, w_proj_a, w_proj_b, w_out, w_ffn_up, conv_w,
           conv_b, w_ffn_down):
    depth = w_in.shape[0]
    d = x_prompt.shape[-1]
    bp, bs = x_prompt.shape[0], x_sample.shape[0]
    off, _ = _in_layout(d)
    n_rw = d // RW_HEAD
    yp, ys = x_prompt, x_sample
    st_p, st_s = [], []
    pad_rows = lambda a, rows: jnp.concatenate([a, jnp.zeros((rows - a.shape[0],) + a.shape[1:], a.dtype)], axis=0)
    for l in range(depth):
        heads = lambda a: a.reshape(n_rw, 1, RW_HEAD)
        mu = shift_mu[l]
        mu_lora = jnp.concatenate(
            [mu[3 * d:3 * d + DECAY_LORA], jnp.zeros((LANES - DECAY_LORA,), F32),
             mu[3 * d + DECAY_LORA:3 * d + DECAY_LORA + ICLR_LORA], jnp.zeros((LANES - ICLR_LORA,), F32),
             mu[3 * d + DECAY_LORA + ICLR_LORA:]])
        W = {
            "off": off,
            "norm1_g": norm1_g[l][None], "norm2_g": norm2_g[l][None],
            "w_in": _pack_cols(w_in[l], d).astype(BF16),
            "mu_packed": jnp.concatenate([mu[:3 * d], mu_lora])[None],
            "w0": w0[l][None], "a0": a0[l][None],
            "w_decay_up": pad_rows(w_decay_up[l], LANES).astype(BF16),
            "w_a_up": pad_rows(w_a_up[l], LANES).astype(BF16),
            "w_g_up": w_g_up[l].astype(BF16),
            "rw_params": [heads(k_k[l]), heads(k_a[l]), heads(r_k[l]), heads(lnx_w[l]), heads(lnx_b[l])],
            "q_norm_g": q_norm_g[l][None], "k_norm_g": k_norm_g[l][None],
            "w_proj_a": w_proj_a[l].astype(BF16), "w_proj_b": w_proj_b[l].astype(BF16),
            "w_out": w_out[l].astype(BF16), "w_ffn_up": w_ffn_up[l].astype(BF16),
            "conv_w": conv_w[l], "conv_b": conv_b[l][None], "w_ffn_down": w_ffn_down[l].astype(BF16),
        }
        c_all = jnp.concatenate([c_prompt, c_sample], axis=0)
        rows = -(-c_all.shape[0] // SUBLANES) * SUBLANES
        mod = _matmul(pad_rows(c_all, rows), w_ada[l].astype(BF16), tm=rows, tn=1024, bias=b_ada[l][None],
                      pre_silu=True, name="ada_mod")
        mod_p, mod_s = mod[:bp], mod[bp:bp + bs]
        d_ff = w_ffn_down.shape[1]
        yp, *sp = _layer(yp, mod_p, jnp.zeros((bp, state_shift.shape[-1]), F32),
                         jnp.zeros((bp, n_rw, RW_HEAD, RW_HEAD), F32), jnp.zeros((bp, CONV_W - 1, d_ff), F32),
                         W, None)
        ys, *ss = _layer(ys, mod_s, state_shift[l], state_rwkv[l], state_conv[l], W,
                         (cache_k[l], cache_v[l], cache_kidx[l], page_table))
        st_p.append(sp)
        st_s.append(ss)
    stk = lambda sts, i: jnp.stack([s[i] for s in sts], axis=0)
    return (yp, ys,
            stk(st_p, 0), stk(st_p, 1), stk(st_p, 2), stk(st_p, 3), stk(st_p, 4), stk(st_p, 5),
            stk(st_s, 0), stk(st_s, 1), stk(st_s, 2), stk(st_s, 3), stk(st_s, 4), stk(st_s, 5))
```

```python
import functools

import numpy as np
import jax
import jax.numpy as jnp
from jax import lax
from jax.experimental import pallas as pl
from jax.experimental.pallas import tpu as pltpu

F32 = jnp.float32
BF16 = jnp.bfloat16
I32 = jnp.int32

NORM_EPS = 1e-6
LNX_EPS = 64e-5
RW_HEAD = 64
ATT_HEAD_DIM = 128
KV_HEADS = 4
IDX_HEADS = 16
IDX_DIM = 64
PAGE_SIZE = 128
TOPK_MAX = 256
CONV_W = 3
DECAY_LORA = 96
ICLR_LORA = 96
GATE_LORA = 256

LANES = 128
SUBLANES = 8
VMEM_LIMIT = 56 * 1024 * 1024
NEG_INF = float("-inf")
INT_MIN = -2 ** 31


def _cparams(n_axes):
    return pltpu.CompilerParams(dimension_semantics=("arbitrary",) * n_axes,
                                vmem_limit_bytes=VMEM_LIMIT)


def _expand_rows(v, tm):
    s, w = v.shape
    if s == 1:
        return v
    return jnp.broadcast_to(v[:, None, :], (s, tm // s, w)).reshape(tm, w)


def _sigmoid(x):
    return 1.0 / (1.0 + jnp.exp(-x))


def _in_layout(d):
    rw = d
    att = d
    kv = KV_HEADS * ATT_HEAD_DIM
    qi = IDX_HEADS * IDX_DIM
    off = {}
    pos = 0
    for name, width in (("ga", d), ("gb", d), ("r", rw), ("k", rw), ("v", rw), ("q", att), ("qi", qi),
                        ("kk", kv), ("vv", kv), ("lora", 512), ("ki2", LANES), ("wi", LANES)):
        assert pos % width == 0, (name, pos, width)
        off[name] = pos
        pos += width
    total = -(-pos // 1024) * 1024
    return off, total


def _pack_cols(w, d):
    rw, att = d, d
    kv = KV_HEADS * ATT_HEAD_DIM
    qi = IDX_HEADS * IDX_DIM
    sizes = (rw, rw, rw, DECAY_LORA, ICLR_LORA, GATE_LORA, att, kv, kv, qi, IDX_DIM, IDX_HEADS, d, d)
    cuts = np.cumsum((0,) + sizes)
    seg = {n: w[..., cuts[i]:cuts[i + 1]] for i, n in enumerate(
        ("r", "k", "v", "wd", "ad", "gd", "q", "kk", "vv", "qi", "ki", "wi", "ga", "gb"))}
    z = lambda n: jnp.zeros(w.shape[:-1] + (n,), w.dtype)
    _, total = _in_layout(d)
    parts = [seg["ga"], seg["gb"], seg["r"], seg["k"], seg["v"], seg["q"], seg["qi"], seg["kk"], seg["vv"],
             seg["wd"], z(LANES - DECAY_LORA), seg["ad"], z(LANES - ICLR_LORA), seg["gd"],
             seg["ki"], seg["ki"], seg["wi"], z(LANES - IDX_HEADS)]
    used = sum(p.shape[-1] for p in parts)
    parts.append(z(total - used))
    return jnp.concatenate(parts, axis=-1)


def _mm_kernel(*refs, pre_silu, has_bias, has_res):
    x_ref, w_ref = refs[0], refs[1]
    o_ref = refs[-1]
    x = x_ref[...]
    if pre_silu:
        x = x.astype(F32)
        x = x * _sigmoid(x)
    acc = jnp.dot(x.astype(BF16), w_ref[...], preferred_element_type=F32)
    nxt = 2
    if has_bias:
        acc = acc + refs[nxt][...]
        nxt += 1
    if has_res:
        res_ref, gate_ref = refs[nxt], refs[nxt + 1]
        acc = res_ref[...] + _expand_rows(gate_ref[...], acc.shape[0]) * acc
    o_ref[...] = acc.astype(o_ref.dtype)


def _matmul(x, w, *, tm, tn, bias=None, pre_silu=False, res=None, gate=None, gate_col=0, seq_len=None,
            out_dtype=F32, name="mm"):
    m, k = x.shape
    n = w.shape[1]
    tm = min(tm, m)
    assert m % tm == 0 and n % tn == 0, (m, tm, n, tn)
    in_specs = [pl.BlockSpec((tm, k), lambda i, j: (i, 0)),
                pl.BlockSpec((k, tn), lambda i, j: (0, j))]
    args = [x, w]
    if bias is not None:
        in_specs.append(pl.BlockSpec((1, tn), lambda i, j: (0, j)))
        args.append(bias)
    if res is not None:
        in_specs.append(pl.BlockSpec((tm, tn), lambda i, j: (i, j)))
        args.append(res)
        gcol = gate_col * (n // tn)
        if seq_len >= tm:
            in_specs.append(pl.BlockSpec((1, tn), lambda i, j: ((i * tm) // seq_len, gcol + j)))
        else:
            in_specs.append(pl.BlockSpec((tm // seq_len, tn), lambda i, j: (i, gcol + j)))
        args.append(gate)
    return pl.pallas_call(
        functools.partial(_mm_kernel, pre_silu=pre_silu, has_bias=bias is not None, has_res=res is not None),
        grid=(m // tm, n // tn),
        in_specs=in_specs,
        out_specs=pl.BlockSpec((tm, tn), lambda i, j: (i, j)),
        out_shape=jax.ShapeDtypeStruct((m, n), out_dtype),
        compiler_params=_cparams(2),
        name=name,
    )(*args)


def _norm_mod_kernel(x_ref, g_ref, sc_ref, sh_ref, o_ref):
    x = x_ref[...]
    tm = x.shape[0]
    y = x * lax.rsqrt(jnp.mean(x * x, axis=-1, keepdims=True) + NORM_EPS)
    y = y * g_ref[...]
    y = y * (1.0 + _expand_rows(sc_ref[...], tm)) + _expand_rows(sh_ref[...], tm)
    o_ref[...] = y.astype(o_ref.dtype)


def _mod_spec(tm, width, col, seq_len):
    if seq_len >= tm:
        return pl.BlockSpec((1, width), lambda i: ((i * tm) // seq_len, col))
    return pl.BlockSpec((tm // seq_len, width), lambda i: (i, col))


def _norm_mod(x, g, mod, sc_col, sh_col, seq_len, tm=256):
    m, d = x.shape
    tm = min(tm, m)
    return pl.pallas_call(
        _norm_mod_kernel,
        grid=(m // tm,),
        in_specs=[pl.BlockSpec((tm, d), lambda i: (i, 0)),
                  pl.BlockSpec((1, d), lambda i: (0, 0)),
                  _mod_spec(tm, d, sc_col, seq_len),
                  _mod_spec(tm, d, sh_col, seq_len)],
        out_specs=pl.BlockSpec((tm, d), lambda i: (i, 0)),
        out_shape=jax.ShapeDtypeStruct((m, d), BF16),
        compiler_params=_cparams(1),
        name="norm_mod",
    )(x, g, mod, mod)


def _prev_rows(x, halo_ref, state_refs, i, seq_len, shift):
    tm = x.shape[0]
    row = lax.broadcasted_iota(I32, (tm, 1), 0)
    out = pltpu.roll(x, shift, 0)
    n_state = len(state_refs)
    for r in range(shift):
        back = shift - r
        state_row = state_refs[n_state - back][...]
        if seq_len >= tm:
            src = jnp.where(i == 0, state_row, halo_ref[SUBLANES - back:SUBLANES - back + 1, :])
            out = jnp.where(row == r, src, out)
        else:
            out = jnp.where(row % seq_len == r, _expand_rows(state_row, tm), out)
    return out


def _halo_spec(tm, width, col):
    return pl.BlockSpec((SUBLANES, width), lambda i: (jnp.maximum(i * (tm // SUBLANES) - 1, 0), col))


def _state_spec(tm, width, col, seq_len):
    if seq_len >= tm:
        return pl.BlockSpec((1, width), lambda i: ((i * tm) // seq_len, col))
    return pl.BlockSpec((tm // seq_len, width), lambda i: (i, col))


def _softplus(x):
    return jnp.maximum(x, 0.0) + jnp.log1p(jnp.exp(-jnp.abs(x)))


def _rwkv_prep_kernel(zr_ref, zk_ref, zv_ref, zl_ref, hr_ref, hk_ref, hv_ref, hl_ref,
                      sr_ref, sk_ref, sv_ref, sl_ref, mur_ref, muk_ref, muv_ref, mul_ref,
                      w0_ref, wdec_ref, a0_ref, wa_ref, wg_ref,
                      r_ref, k_ref, v_ref, dec_ref, ic_ref, g_ref, *, seq_len):
    i = pl.program_id(0)

    def shifted(z_ref, h_ref, s_ref, mu_ref):
        x = z_ref[...]
        prev = _prev_rows(x, h_ref, [s_ref], i, seq_len, 1)
        return x + (prev - x) * mu_ref[...]

    r_ref[...] = shifted(zr_ref, hr_ref, sr_ref, mur_ref)
    k_ref[...] = shifted(zk_ref, hk_ref, sk_ref, muk_ref)
    v_ref[...] = shifted(zv_ref, hv_ref, sv_ref, muv_ref)
    zl = shifted(zl_ref, hl_ref, sl_ref, mul_ref)
    wd = zl[:, 0:LANES]
    ad = zl[:, LANES:2 * LANES]
    gd = zl[:, 2 * LANES:4 * LANES]
    mm = lambda a, w_ref: jnp.dot(a.astype(BF16), w_ref[...], preferred_element_type=F32)
    dec_in = w0_ref[...] + mm(jnp.tanh(wd), wdec_ref)
    w_log = -_softplus(-dec_in) - 0.5
    dec_ref[...] = -jnp.exp(w_log)
    ic_ref[...] = _sigmoid(a0_ref[...] + mm(ad, wa_ref))
    g_ref[...] = mm(_sigmoid(gd), wg_ref)


def _rwkv_prep(z, off, state_packed, mu_packed, w0, wdec, a0, wa, wg, seq_len, d, tm=256):
    m = z.shape[0]
    tm = min(tm, m)
    cb = lambda name, width: off[name] // width
    zspec = lambda name, width: pl.BlockSpec((tm, width), lambda i: (i, cb(name, width)))
    sspec = lambda c, width: _state_spec(tm, width, c, seq_len)
    mspec = lambda c, width: pl.BlockSpec((1, width), lambda i: (0, c))
    full = lambda a: pl.BlockSpec(a.shape, lambda i: (0,) * a.ndim)
    lcol = 3 * d // 512
    out_spec = pl.BlockSpec((tm, d), lambda i: (i, 0))
    out_sds = jax.ShapeDtypeStruct((m, d), F32)
    return pl.pallas_call(
        functools.partial(_rwkv_prep_kernel, seq_len=seq_len),
        grid=(m // tm,),
        in_specs=[zspec("r", d), zspec("k", d), zspec("v", d), zspec("lora", 512),
                  _halo_spec(tm, d, cb("r", d)), _halo_spec(tm, d, cb("k", d)), _halo_spec(tm, d, cb("v", d)),
                  _halo_spec(tm, 512, cb("lora", 512)),
                  sspec(0, d), sspec(1, d), sspec(2, d), sspec(lcol, 512),
                  mspec(0, d), mspec(1, d), mspec(2, d), mspec(lcol, 512),
                  full(w0), full(wdec), full(a0), full(wa), full(wg)],
        out_specs=[out_spec] * 6,
        out_shape=[out_sds] * 6,
        compiler_params=_cparams(1),
        name="rwkv_prep",
    )(z, z, z, z, z, z, z, z, state_packed, state_packed, state_packed, state_packed,
      mu_packed, mu_packed, mu_packed, mu_packed, w0, wdec, a0, wa, wg)


def _scan_kernel(r_ref, logw_ref, k_ref, v_ref, ic_ref, g_ref, s0_ref, kk_ref, ka_ref, rk_ref, lw_ref, lb_ref,
                 o_ref, sout_ref, s_scr, a_scr, b_scr, km_scr, o_scr, w_ref, *, gsub):
    ti = pl.program_id(1)
    gb, tb, n = r_ref.shape

    @pl.when(ti == 0)
    def _():
        s_scr[...] = s0_ref[...]

    for g in range(gb):
        kraw = k_ref[g]
        ic = ic_ref[g]
        kk = kraw * kk_ref[g]
        nrm = jnp.sqrt(jnp.sum(kk * kk, axis=-1, keepdims=True))
        kk = kk / jnp.maximum(nrm, 1e-12)
        a_scr[g] = -kk
        b_scr[g] = kk * ic
        km_scr[g] = kraw * (1.0 + (ic - 1.0) * ka_ref[g])
        w_ref[g] = jnp.exp(logw_ref[g])

    eye =(lax.broadcasted_iota(I32, (n, n), 0) == lax.broadcasted_iota(I32, (n, n), 1)).astype(F32)

    for g0 in range(0, gb, gsub):
        heads = range(g0, min(g0 + gsub, gb))

        def step(t, carry):
            new = []
            for s, g in zip(carry, heads):
                row = lambda ref: ref[g, pl.ds(t, 1), :]
                sa = jnp.sum(s * row(a_scr), axis=1, keepdims=True)
                vcol = jnp.sum(eye * row(v_ref), axis=1, keepdims=True)
                s = s * row(w_ref) + sa * row(b_scr) + vcol * row(km_scr)
                ocol = jnp.sum(s * row(r_ref), axis=1, keepdims=True)
                o_scr[g, pl.ds(t, 1), :] = jnp.sum(eye * ocol, axis=0, keepdims=True)
                new.append(s)
            return tuple(new)

        fin = lax.fori_loop(0, tb, step, tuple(s_scr[g] for g in heads))
        for s, g in zip(fin, heads):
            s_scr[g] = s

    for g in range(gb):
        o = o_scr[g]
        mu = jnp.mean(o, axis=-1, keepdims=True)
        var = jnp.mean(jnp.square(o - mu), axis=-1, keepdims=True)
        on = (o - mu) * lax.rsqrt(var + LNX_EPS) * lw_ref[g] + lb_ref[g]
        bonus = jnp.sum(r_ref[g] * km_scr[g] * rk_ref[g], axis=-1, keepdims=True) * v_ref[g]
        o_ref[g] = (on + bonus) * g_ref[g]

    sout_ref[...] = s_scr[...]


def _rwkv_scan(seqs, s0, params, gb, tb, gsub=4):
    g_tot, t_tot, n = seqs[0].shape
    n_heads = params[0].shape[0]
    assert g_tot % gb == 0 and t_tot % tb == 0 and n_heads % gb == 0
    hblocks = n_heads // gb
    seq_spec = pl.BlockSpec((gb, tb, n), lambda gi, ti: (gi, ti, 0))
    st_spec = pl.BlockSpec((gb, n, n), lambda gi, ti: (gi, 0, 0))
    par_spec = pl.BlockSpec((gb, 1, n), lambda gi, ti: (gi % hblocks, 0, 0))
    return pl.pallas_call(
        functools.partial(_scan_kernel, gsub=gsub),
        grid=(g_tot // gb, t_tot // tb),
        in_specs=[seq_spec] * 6 + [st_spec] + [par_spec] * 5,
        out_specs=[seq_spec, st_spec],
        out_shape=[jax.ShapeDtypeStruct((g_tot, t_tot, n), F32), jax.ShapeDtypeStruct((g_tot, n, n), F32)],
        scratch_shapes=[pltpu.VMEM((gb, n, n), F32)] + [pltpu.VMEM((gb, tb, n), F32)] * 5,
        compiler_params=_cparams(2),
        name="rwkv_scan",
    )(*seqs, s0, *params)


SCAN_CHUNK = 64


def _split_bf16(x):
    hi = x.astype(BF16)
    return hi, (x - hi.astype(F32)).astype(BF16)


def _dot3(a, b, contract):
    dg = lambda x, y: lax.dot_general(x, y, (contract, ((), ())), preferred_element_type=F32)
    return dg(a[0], b[0]) + (dg(a[0], b[1]) + dg(a[1], b[0]))


_NN = ((1,), (0,))
_NT = ((1,), (1,))
_TN = ((0,), (0,))


def _scan_chunk_kernel(r_ref, logw_ref, k_ref, v_ref, ic_ref, g_ref, kk_ref, ka_ref, rk_ref, lw_ref, lb_ref,
                       o_ref, sout_ref, sb_scr, o_scr, km_scr):
    ti = pl.program_id(1)
    tb = r_ref.shape[0]
    c = SCAN_CHUNK
    n = RW_HEAD

    @pl.when(ti == 0)
    def _():
        sb_scr[...] = jnp.zeros_like(sb_scr)

    lane = lax.broadcasted_iota(I32, (1, LANES), 1)
    head0 = lane < n
    rowc = lax.broadcasted_iota(I32, (c, 1), 0)

    def hsum(x):
        s0 = jnp.sum(jnp.where(head0, x, 0.0), axis=1, keepdims=True)
        s1 = jnp.sum(jnp.where(head0, 0.0, x), axis=1, keepdims=True)
        return jnp.where(head0, s0, s1)

    def stack(x):
        return jnp.concatenate([jnp.where(head0, x, 0.0), jnp.where(head0, 0.0, x)], axis=0)

    rl = lax.broadcasted_iota(I32, (2 * c, 2 * c), 0) % c
    cl = lax.broadcasted_iota(I32, (2 * c, 2 * c), 1) % c
    strict = cl < rl
    incl = cl <= rl
    eye = (lax.broadcasted_iota(I32, (2 * c, 2 * c), 0) == lax.broadcasted_iota(I32, (2 * c, 2 * c), 1)).astype(F32)

    pre = []
    for ci in range(tb // c):
        sl = slice(ci * c, (ci + 1) * c)
        r, logw, kraw, v, ic = r_ref[sl, :], logw_ref[sl, :], k_ref[sl, :], v_ref[sl, :], ic_ref[sl, :]
        kk = kraw * kk_ref[...]
        kk = kk / jnp.maximum(jnp.sqrt(hsum(kk * kk)), 1e-12)
        kmod = kraw * (1.0 + (ic - 1.0) * ka_ref[...])
        km_scr[sl, :] = kmod
        cum = logw
        sh = 1
        while sh < c:
            cum = cum + jnp.where(rowc >= sh, pltpu.roll(cum, sh, 0), 0.0)
            sh *= 2
        p_inv = jnp.exp(-cum)
        a_t = -kk * jnp.exp(cum - logw)
        r_t = r * jnp.exp(cum)
        b_t = kk * ic * p_inv
        k_t = kmod * p_inv
        p_end = jnp.exp(cum[c - 1:c, :])
        ar = _split_bf16(jnp.concatenate([stack(a_t), stack(r_t)], axis=0))
        bk = _split_bf16(jnp.concatenate([stack(b_t), stack(k_t)], axis=0))
        vst = stack(v)
        g4 = _dot3(ar, bk, _NT)
        a_ab = jnp.where(strict, g4[0:2 * c, 0:2 * c], 0.0)
        a_ak = jnp.where(strict, g4[0:2 * c, 2 * c:4 * c], 0.0)
        a_rb = jnp.where(incl, g4[2 * c:4 * c, 0:2 * c], 0.0)
        a_rk = jnp.where(incl, g4[2 * c:4 * c, 2 * c:4 * c], 0.0)
        mpow = a_ab
        tinv = eye + a_ab
        span = 2
        while span < c:
            ms = _split_bf16(mpow)
            mpow = _dot3(ms, ms, _NN)
            tinv = tinv + _dot3(_split_bf16(tinv), _split_bf16(mpow), _NN)
            span *= 2
        vs = _split_bf16(vst)
        pre.append(dict(ar=ar, bk=bk, vs=vs, vst=vst, tinv=_split_bf16(tinv), a_rb=_split_bf16(a_rb),
                        akv=_dot3(_split_bf16(a_ak), vs, _NN), rkv=_dot3(_split_bf16(a_rk), vs, _NN), p_end=p_end))

    sb = sb_scr[...]
    for ci, p in enumerate(pre):
        xr = _dot3(p["ar"], _split_bf16(sb), _NT)
        ust = _dot3(p["tinv"], _split_bf16(xr[0:2 * c] + p["akv"]), _NN)
        us = _split_bf16(ust)
        ost = xr[2 * c:4 * c] + _dot3(p["a_rb"], us, _NN) + p["rkv"]
        o_scr[ci * c:(ci + 1) * c, :] = ost[0:c] + ost[c:2 * c]
        uv = (jnp.concatenate([us[0], p["vs"][0]], axis=0), jnp.concatenate([us[1], p["vs"][1]], axis=0))
        sb = (sb + _dot3(uv, p["bk"], _TN)) * p["p_end"]
    sb_scr[...] = sb
    sout_ref[0] = sb

    o = o_scr[...]
    mu = hsum(o) * (1.0 / n)
    var = hsum(jnp.square(o - mu)) * (1.0 / n)
    on = (o - mu) * lax.rsqrt(var + LNX_EPS) * lw_ref[...] + lb_ref[...]
    bonus = hsum(r_ref[...] * km_scr[...] * rk_ref[...]) * v_ref[...]
    o_ref[...] = ((on + bonus) * g_ref[...]).astype(o_ref.dtype)


def _rwkv_scan_chunked(seqs, params, tb=512):
    t_tot, d = seqs[0].shape
    tb = min(tb, t_tot)
    assert t_tot % tb == 0 and tb % SCAN_CHUNK == 0 and d % LANES == 0
    n_pairs = d // LANES
    seq_spec = pl.BlockSpec((tb, LANES), lambda hp, ti: (ti, hp))
    par_spec = pl.BlockSpec((1, LANES), lambda hp, ti: (0, hp))
    return pl.pallas_call(
        _scan_chunk_kernel,
        grid=(n_pairs, t_tot // tb),
        in_specs=[seq_spec] * 6 + [par_spec] * 5,
        out_specs=[seq_spec, pl.BlockSpec((1, LANES, LANES), lambda hp, ti: (hp, 0, 0))],
        out_shape=[jax.ShapeDtypeStruct((t_tot, d), BF16), jax.ShapeDtypeStruct((n_pairs, LANES, LANES), F32)],
        scratch_shapes=[pltpu.VMEM((LANES, LANES), F32), pltpu.VMEM((tb, LANES), F32), pltpu.VMEM((tb, LANES), F32)],
        compiler_params=_cparams(2),
        name="rwkv_scan_chunked",
    )(*seqs, *params)


def _dsa_prep_kernel(q_ref, k_ref, qg_ref, kg_ref, qo_ref, ko_ref, kb_ref):
    def head_norm(x, g):
        outs = []
        for h in range(x.shape[1] // ATT_HEAD_DIM):
            xh = x[:, h * ATT_HEAD_DIM:(h + 1) * ATT_HEAD_DIM]
            outs.append(xh * lax.rsqrt(jnp.mean(xh * xh, axis=-1, keepdims=True) + NORM_EPS) * g)
        return jnp.concatenate(outs, axis=1)

    qo_ref[...] = head_norm(q_ref[...], qg_ref[...]).astype(qo_ref.dtype)
    kn = head_norm(k_ref[...], kg_ref[...])
    ko_ref[...] = kn
    kb_ref[...] = kn.astype(kb_ref.dtype)


def _dsa_prep(z, off, qg, kg, d, tm=256):
    m = z.shape[0]
    tm = min(tm, m)
    kv = KV_HEADS * ATT_HEAD_DIM
    return pl.pallas_call(
        _dsa_prep_kernel,
        grid=(m // tm,),
        in_specs=[pl.BlockSpec((tm, d), lambda i: (i, off["q"] // d)),
                  pl.BlockSpec((tm, kv), lambda i: (i, off["kk"] // kv)),
                  pl.BlockSpec((1, ATT_HEAD_DIM), lambda i: (0, 0)),
                  pl.BlockSpec((1, ATT_HEAD_DIM), lambda i: (0, 0))],
        out_specs=[pl.BlockSpec((tm, d), lambda i: (i, 0)),
                   pl.BlockSpec((tm, kv), lambda i: (i, 0)),
                   pl.BlockSpec((tm, kv), lambda i: (i, 0))],
        out_shape=[jax.ShapeDtypeStruct((m, d), BF16), jax.ShapeDtypeStruct((m, kv), F32),
                   jax.ShapeDtypeStruct((m, kv), BF16)],
        compiler_params=_cparams(1),
        name="dsa_prep",
    )(z, z, qg, kg)


def _sort_key(x):
    x = jnp.where(x == 0.0, 0.0, x)
    bits = pltpu.bitcast(x, I32)
    return bits ^ (lax.shift_right_arithmetic(bits, 31) & 0x7FFFFFFF)


def _topk_threshold(skey_ref, n_blocks, k_top, rows):
    def count(pred_fn):
        def body(j, acc):
            return acc + jnp.where(pred_fn(skey_ref[j]), 1.0, 0.0)
        acc = lax.fori_loop(0, n_blocks, body, jnp.zeros((rows, LANES), F32))
        return jnp.sum(acc, axis=1, keepdims=True)

    def bit_body(it, tau):
        cand = tau + lax.shift_left(jnp.int32(1), 31 - it)
        return jnp.where(count(lambda s: s >= cand) >= k_top, cand, tau)

    tau = lax.fori_loop(0, 32, bit_body, jnp.full((rows, 1), INT_MIN, I32))
    need = k_top - count(lambda s: s > tau)
    return tau, need


def _selected(skey, tau, need, eq_before, tri):
    eq = skey == tau
    eqf = jnp.where(eq, 1.0, 0.0)
    prefix = eq_before + jnp.dot(eqf.astype(BF16), tri, preferred_element_type=F32)
    sel = (skey > tau) | (eq & (prefix <= need))
    return sel, eq_before + jnp.sum(eqf, axis=1, keepdims=True)


def _tri_incl():
    return (lax.broadcasted_iota(I32, (LANES, LANES), 0) <= lax.broadcasted_iota(I32, (LANES, LANES), 1)).astype(BF16)


def _sel_prompt_kernel(qi_ref, wi_ref, ki_ref, bias_ref, skey_scr, qm_scr, wb_scr, *, k_top):
    i = pl.program_id(0)
    qb = qi_ref.shape[0]
    nkb = bias_ref.shape[1] // LANES
    kcols = lambda j: pl.ds(pl.multiple_of(j * LANES, LANES), LANES)
    lane = lax.broadcasted_iota(I32, (qb, LANES), 1)
    rowq = lax.broadcasted_iota(I32, (qb, LANES), 0)
    wi = wi_ref[...] * (IDX_HEADS ** -0.5)
    for hp in range(IDX_HEADS // 2):
        qpair = qi_ref[:, hp * LANES:(hp + 1) * LANES]
        qm_scr[2 * hp] = jnp.where(lane < IDX_DIM, qpair, 0.0).astype(BF16)
        qm_scr[2 * hp + 1] = jnp.where(lane >= IDX_DIM, qpair, 0.0).astype(BF16)
    for h in range(IDX_HEADS):
        wb_scr[h] = jnp.broadcast_to(wi[:, h:h + 1], (qb, LANES))

    def score_block(j, carry):
        kblk = ki_ref[pl.ds(pl.multiple_of(j * LANES, LANES), LANES), :].astype(BF16)
        sc = jnp.zeros((qb, LANES), F32)
        for h in range(IDX_HEADS):
            dots = lax.dot_general(qm_scr[h], kblk, (((1,), (1,)), ((), ())), preferred_element_type=F32)
            sc = sc + jnp.maximum(dots * (IDX_DIM ** -0.5), 0.0) * wb_scr[h]
        allowed = (j < i) | (lane <= rowq)
        skey_scr[j] = _sort_key(jnp.where(allowed, sc, NEG_INF))
        return carry

    lax.fori_loop(0, i + 1, score_block, 0)
    tau, need = _topk_threshold(skey_scr, i + 1, float(k_top), qb)
    tri = _tri_incl()

    def write_block(j, eq_before):
        sel, eq_after = _selected(skey_scr[j], tau, need, eq_before, tri)
        allowed = (j < i) | (lane <= rowq)
        bias_ref[:, kcols(j)] = jnp.where(sel & allowed, 0.0, NEG_INF).astype(bias_ref.dtype)
        return eq_after

    lax.fori_loop(0, i + 1, write_block, jnp.zeros((qb, 1), F32))

    def fill_block(j, carry):
        bias_ref[:, kcols(j)] = jnp.full((qb, LANES), NEG_INF, bias_ref.dtype)
        return carry

    lax.fori_loop(i + 1, nkb, fill_block, 0)


def _sel_prompt(z, off, k_top):
    t = z.shape[0]
    qb = LANES
    nkb = t // LANES
    nqb = t // qb
    qi_w = IDX_HEADS * IDX_DIM
    return pl.pallas_call(
        functools.partial(_sel_prompt_kernel, k_top=k_top),
        grid=(nqb,),
        in_specs=[pl.BlockSpec((qb, qi_w), lambda i: (i, off["qi"] // qi_w)),
                  pl.BlockSpec((qb, LANES), lambda i: (i, off["wi"] // LANES)),
                  pl.BlockSpec((t, LANES), lambda i: (0, off["ki2"] // LANES))],
        out_specs=pl.BlockSpec((qb, t), lambda i: (i, 0)),
        out_shape=jax.ShapeDtypeStruct((t, t), BF16),
        scratch_shapes=[pltpu.VMEM((nkb, qb, LANES), I32),
                        pltpu.VMEM((IDX_HEADS, qb, LANES), BF16),
                        pltpu.VMEM((IDX_HEADS, qb, LANES), F32)],
        compiler_params=_cparams(1),
        name="dsa_select_prompt",
    )(z, z, z)


ATT_TILE = 256


def _att_prompt_kernel(qidx_ref, kidx_ref, q_ref, k_ref, v_ref, bias_ref, o_ref, m_scr, l_scr, acc_scr):
    p = pl.program_id(0)
    qi = qidx_ref[p]
    kj = kidx_ref[p]
    tq = q_ref.shape[0]
    n_heads = q_ref.shape[1] // ATT_HEAD_DIM
    group = n_heads // KV_HEADS
    first = kj == 0
    bias = bias_ref[...].astype(F32)
    bias = jnp.concatenate([bias] * group, axis=0)
    for n in range(KV_HEADS):
        cols = lambda h: slice(h * ATT_HEAD_DIM, (h + 1) * ATT_HEAD_DIM)
        q = jnp.concatenate([q_ref[:, cols(n * group + g)] for g in range(group)], axis=0)
        s = lax.dot_general(q, k_ref[:, cols(n)], (_NT, ((), ())), preferred_element_type=F32)
        s = s * (ATT_HEAD_DIM ** -0.5) + bias
        m_prev = jnp.where(first, NEG_INF, m_scr[n])
        l_prev = jnp.where(first, 0.0, l_scr[n])
        acc_prev = jnp.where(first, 0.0, acc_scr[n])
        m_cur = jnp.maximum(m_prev, jnp.max(s, axis=1, keepdims=True))
        m_safe = jnp.where(m_cur == NEG_INF, 0.0, m_cur)
        pexp = jnp.exp(s - jnp.concatenate([m_safe] * (s.shape[1] // LANES), axis=1))
        alpha = jnp.exp(m_prev - m_safe)
        l_scr[n] = alpha * l_prev + jnp.sum(pexp, axis=1, keepdims=True)
        acc_scr[n] = alpha * acc_prev + jnp.dot(pexp.astype(BF16), v_ref[:, cols(n)], preferred_element_type=F32)
        m_scr[n] = m_cur

    @pl.when(kj == qi)
    def _():
        for n in range(KV_HEADS):
            res = acc_scr[n] / l_scr[n]
            for g in range(group):
                h = n * group + g
                o_ref[:, h * ATT_HEAD_DIM:(h + 1) * ATT_HEAD_DIM] = res[g * tq:(g + 1) * tq].astype(o_ref.dtype)


def _att_prompt(q, kb, vb, bias):
    t, d = q.shape
    tile = min(ATT_TILE, t)
    kv = kb.shape[1]
    nqb = t // tile
    pairs = [(i, j) for i in range(nqb) for j in range(i + 1)]
    qidx = jnp.asarray(np.array([p[0] for p in pairs], np.int32))
    kidx = jnp.asarray(np.array([p[1] for p in pairs], np.int32))
    group = d // ATT_HEAD_DIM // KV_HEADS
    grid_spec = pltpu.PrefetchScalarGridSpec(
        num_scalar_prefetch=2,
        grid=(len(pairs),),
        in_specs=[pl.BlockSpec((tile, d), lambda p, qi, kj: (qi[p], 0)),
                  pl.BlockSpec((tile, kv), lambda p, qi, kj: (kj[p], 0)),
                  pl.BlockSpec((tile, kv), lambda p, qi, kj: (kj[p], 0)),
                  pl.BlockSpec((tile, tile), lambda p, qi, kj: (qi[p], kj[p]))],
        out_specs=pl.BlockSpec((tile, d), lambda p, qi, kj: (qi[p], 0)),
        scratch_shapes=[pltpu.VMEM((KV_HEADS, group * tile, ATT_HEAD_DIM), F32)] * 3,
    )
    return pl.pallas_call(
        _att_prompt_kernel,
        grid_spec=grid_spec,
        out_shape=jax.ShapeDtypeStruct((t, d), BF16),
        compiler_params=_cparams(1),
        name="dsa_attn_prompt",
    )(qidx, kidx, q, kb, vb, bias)


def _dsa_sample_kernel(pt_ref, qall_ref, wcol_ref, qg_ref, knew_i_ref, knew_ref, vnew_ref, *rest, n_pages, k_top):
    ki_pages = rest[:n_pages]
    k_pages = rest[n_pages:2 * n_pages]
    v_pages = rest[2 * n_pages:3 * n_pages]
    o_ref, skey_scr, k_scr, v_scr = rest[3 * n_pages:]
    t_new = knew_ref.shape[1]
    rows = qall_ref.shape[1]
    group_rows = qg_ref.shape[2]
    lane = lax.broadcasted_iota(I32, (t_new, LANES), 1)
    rowq = lax.broadcasted_iota(I32, (t_new, LANES), 0)
    new_ok = lane <= rowq

    def pad_page(x):
        return jnp.concatenate([x, jnp.zeros((PAGE_SIZE - t_new, x.shape[1]), x.dtype)], axis=0)

    qa = qall_ref[0].astype(BF16)
    wc = wcol_ref[0] * (IDX_HEADS ** -0.5)

    def block_scores(kblk):
        dots = lax.dot_general(qa, kblk.astype(BF16), (_NT, ((), ())), preferred_element_type=F32)
        part = jnp.maximum(dots * (IDX_DIM ** -0.5), 0.0) * wc
        sc = part[0:t_new]
        for h in range(1, rows // t_new):
            sc = sc + part[h * t_new:(h + 1) * t_new]
        return sc

    for p in range(n_pages):
        skey_scr[p] = _sort_key(block_scores(ki_pages[p][0]))
        k_scr[p * PAGE_SIZE:(p + 1) * PAGE_SIZE, :] = k_pages[p][0].astype(BF16)
        v_scr[p * PAGE_SIZE:(p + 1) * PAGE_SIZE, :] = v_pages[p][0].astype(BF16)
    skey_scr[n_pages] = _sort_key(jnp.where(new_ok, block_scores(pad_page(knew_i_ref[0])), NEG_INF))
    k_scr[n_pages * PAGE_SIZE:(n_pages + 1) * PAGE_SIZE, :] = pad_page(knew_ref[0]).astype(BF16)
    v_scr[n_pages * PAGE_SIZE:(n_pages + 1) * PAGE_SIZE, :] = pad_page(vnew_ref[0]).astype(BF16)

    tau, need = _topk_threshold(skey_scr, n_pages + 1, float(k_top), t_new)
    tri = _tri_incl()
    eq_before = jnp.zeros((t_new, 1), F32)
    bias = []
    for p in range(n_pages + 1):
        sel, eq_before = _selected(skey_scr[p], tau, need, eq_before, tri)
        ok = sel if p < n_pages else sel & new_ok
        bias.append(jnp.where(ok, 0.0, NEG_INF))
    bias = jnp.concatenate(bias, axis=1)
    bias = jnp.concatenate([bias] * (group_rows // t_new), axis=0)

    for n in range(KV_HEADS):
        cols = slice(n * ATT_HEAD_DIM, (n + 1) * ATT_HEAD_DIM)
        s = lax.dot_general(qg_ref[0, n], k_scr[:, cols], (_NT, ((), ())), preferred_element_type=F32)
        s = s * (ATT_HEAD_DIM ** -0.5) + bias
        pexp = jnp.exp(s - jnp.max(s, axis=1, keepdims=True))
        acc = jnp.dot(pexp.astype(BF16), v_scr[:, cols], preferred_element_type=F32)
        o_ref[0, n] = acc / jnp.sum(pexp, axis=1, keepdims=True)


def _dsa_sample(page_table, qall, wcol, qg, knew_i, knew, vnew, cache_kidx, cache_k, cache_v, k_top):
    b, n_pages = page_table.shape
    t_new = knew.shape[1]
    kv = knew.shape[2]
    rows = qall.shape[1]
    group_rows = qg.shape[2]
    per_seq = lambda shape: pl.BlockSpec((1,) + shape, lambda bi, pt: (bi,) + (0,) * len(shape))
    page = lambda width, p: pl.BlockSpec((1, PAGE_SIZE, width), lambda bi, pt: (pt[bi, p], 0, 0))
    grid_spec = pltpu.PrefetchScalarGridSpec(
        num_scalar_prefetch=1,
        grid=(b,),
        in_specs=[per_seq((rows, IDX_DIM)), per_seq((rows, 1)), per_seq((KV_HEADS, group_rows, ATT_HEAD_DIM)),
                  per_seq((t_new, IDX_DIM)), per_seq((t_new, kv)), per_seq((t_new, kv))]
                 + [page(IDX_DIM, p) for p in range(n_pages)]
                 + [page(kv, p) for p in range(n_pages)] * 2,
        out_specs=per_seq((KV_HEADS, group_rows, ATT_HEAD_DIM)),
        scratch_shapes=[pltpu.VMEM((n_pages + 1, t_new, LANES), I32),
                        pltpu.VMEM(((n_pages + 1) * PAGE_SIZE, kv), BF16),
                        pltpu.VMEM(((n_pages + 1) * PAGE_SIZE, kv), BF16)],
    )
    return pl.pallas_call(
        functools.partial(_dsa_sample_kernel, n_pages=n_pages, k_top=k_top),
        grid_spec=grid_spec,
        out_shape=jax.ShapeDtypeStruct((b, KV_HEADS, group_rows, ATT_HEAD_DIM), F32),
        compiler_params=_cparams(1),
        name="dsa_sample",
    )(page_table, qall, wcol, qg, knew_i, knew, vnew,
      *([cache_kidx] * n_pages), *([cache_k] * n_pages), *([cache_v] * n_pages))


def _mix_kernel(xa_ref, xb_ref, wa_ref, wb_ref, ga_ref, gb_ref, o_ref):
    pa = jnp.dot(xa_ref[...], wa_ref[...], preferred_element_type=F32)
    pb = jnp.dot(xb_ref[...], wb_ref[...], preferred_element_type=F32)
    o_ref[...] = (_sigmoid(ga_ref[...]) * pa + _sigmoid(gb_ref[...]) * pb).astype(o_ref.dtype)


def _mix(o_rw, o_att, wa, wb, z, off, d, tm=512, tn=512):
    m = o_rw.shape[0]
    tm = min(tm, m)
    nb = d // tn
    return pl.pallas_call(
        _mix_kernel,
        grid=(m // tm, nb),
        in_specs=[pl.BlockSpec((tm, d), lambda i, j: (i, 0)),
                  pl.BlockSpec((tm, d), lambda i, j: (i, 0)),
                  pl.BlockSpec((d, tn), lambda i, j: (0, j)),
                  pl.BlockSpec((d, tn), lambda i, j: (0, j)),
                  pl.BlockSpec((tm, tn), lambda i, j: (i, off["ga"] // tn + j)),
                  pl.BlockSpec((tm, tn), lambda i, j: (i, off["gb"] // tn + j))],
        out_specs=pl.BlockSpec((tm, tn), lambda i, j: (i, j)),
        out_shape=jax.ShapeDtypeStruct((m, d), BF16),
        compiler_params=_cparams(2),
        name="branch_mix",
    )(o_rw, o_att, wa, wb, z, z)


def _conv_kernel(ug_ref, uv_ref, halo_ref, st0_ref, st1_ref, cw_ref, cb_ref, o_ref, *, seq_len):
    i = pl.program_id(0)
    x = ug_ref[...]
    acc = cb_ref[...] + x * cw_ref[CONV_W - 1:CONV_W, :]
    for back in range(1, CONV_W):
        prev = _prev_rows(x, halo_ref, [st0_ref, st1_ref], i, seq_len, back)
        acc = acc + prev * cw_ref[CONV_W - 1 - back:CONV_W - back, :]
    o_ref[...] = (acc * _sigmoid(acc) * uv_ref[...]).astype(o_ref.dtype)


def _conv_ffn(u, conv_prev, conv_w, conv_b, seq_len, d_ff, tm=256, tn=512):
    m = u.shape[0]
    tm = min(tm, m)
    nb = d_ff // tn
    assert conv_prev.shape[1] == CONV_W - 1 == 2
    if seq_len >= tm:
        st_spec = pl.BlockSpec((1, tn), lambda i, j: ((i * tm) // seq_len, j))
    else:
        st_spec = pl.BlockSpec((tm // seq_len, tn), lambda i, j: (i, j))
    return pl.pallas_call(
        functools.partial(_conv_kernel, seq_len=seq_len),
        grid=(m // tm, nb),
        in_specs=[pl.BlockSpec((tm, tn), lambda i, j: (i, j)),
                  pl.BlockSpec((tm, tn), lambda i, j: (i, nb + j)),
                  pl.BlockSpec((SUBLANES, tn), lambda i, j: (jnp.maximum(i * (tm // SUBLANES) - 1, 0), j)),
                  st_spec, st_spec,
                  pl.BlockSpec((CONV_W, tn), lambda i, j: (0, j)),
                  pl.BlockSpec((1, tn), lambda i, j: (0, j))],
        out_specs=pl.BlockSpec((tm, tn), lambda i, j: (i, j)),
        out_shape=jax.ShapeDtypeStruct((m, d_ff), BF16),
        compiler_params=_cparams(2),
        name="conv_ffn",
    )(u, u, u, conv_prev[:, 0], conv_prev[:, 1], conv_w, conv_b)


def _head_major(x, n_seq, seq_len, n_heads):
    return x.reshape(n_seq, seq_len, n_heads, RW_HEAD).transpose(0, 2, 1, 3).reshape(n_seq * n_heads, seq_len, RW_HEAD)


def _layer(x, mod, shift_prev, s0, conv_prev, W, cache):
    n_seq, seq_len, d = x.shape
    m = n_seq * seq_len
    off = W["off"]
    n_rw = d // RW_HEAD
    kv = KV_HEADS * ATT_HEAD_DIM
    xf = x.reshape(m, d)

    h = _norm_mod(xf, W["norm1_g"], mod, 1, 0, seq_len)
    z = _matmul(h, W["w_in"], tm=1024, tn=1024, name="in_proj")

    lora = lambda a: jnp.concatenate(
        [a[..., 3 * d:3 * d + DECAY_LORA], jnp.zeros(a.shape[:-1] + (LANES - DECAY_LORA,), a.dtype),
         a[..., 3 * d + DECAY_LORA:3 * d + DECAY_LORA + ICLR_LORA],
         jnp.zeros(a.shape[:-1] + (LANES - ICLR_LORA,), a.dtype),
         a[..., 3 * d + DECAY_LORA + ICLR_LORA:]], axis=-1)
    pack_rw = lambda a: jnp.concatenate([a[..., :3 * d], lora(a)], axis=-1)
    state_packed = pack_rw(shift_prev)
    r, k_raw, v_rw, decay, iclr, gate = _rwkv_prep(
        z, off, state_packed, W["mu_packed"], W["w0"], W["w_decay_up"], W["a0"], W["w_a_up"], W["w_g_up"],
        seq_len, d)
    if cache is None:
        assert n_seq == 1 and seq_len % SCAN_CHUNK == 0
        o_rw, s_pairs = _rwkv_scan_chunked((r, decay, k_raw, v_rw, iclr, gate), W["rw_rows"])
        s_new = jnp.stack([s_pairs[:, :RW_HEAD, :RW_HEAD], s_pairs[:, RW_HEAD:, RW_HEAD:]], axis=1).reshape(
            n_seq, n_rw, RW_HEAD, RW_HEAD)
    else:
        seqs = [_head_major(a, n_seq, seq_len, n_rw) for a in (r, decay, k_raw, v_rw, iclr, gate)]
        o_heads, s_new = _rwkv_scan(seqs, s0.reshape(n_seq * n_rw, RW_HEAD, RW_HEAD), W["rw_params"], n_rw, seq_len)
        o_rw = o_heads.reshape(n_seq, n_rw, seq_len, RW_HEAD).transpose(0, 2, 1, 3).reshape(m, d).astype(BF16)
        s_new = s_new.reshape(n_seq, n_rw, RW_HEAD, RW_HEAD)
    lora_cols = z[:, off["lora"]:off["lora"] + 512].reshape(n_seq, seq_len, 512)[:, -1]
    shift_new = jnp.concatenate(
        [z[:, off["r"]:off["r"] + 3 * d].reshape(n_seq, seq_len, 3 * d)[:, -1],
         lora_cols[:, :DECAY_LORA], lora_cols[:, LANES:LANES + ICLR_LORA], lora_cols[:, 2 * LANES:]], axis=-1)

    qn, k_new, k_bf = _dsa_prep(z, off, W["q_norm_g"], W["k_norm_g"], d)
    v_new = z[:, off["vv"]:off["vv"] + kv]
    ki_new = z[:, off["ki2"]:off["ki2"] + IDX_DIM]
    if cache is None:
        assert n_seq == 1
        k_top = min(TOPK_MAX, seq_len // 4)
        bias = _sel_prompt(z, off, k_top)
        o_att = _att_prompt(qn, k_bf, v_new.astype(BF16), bias)
    else:
        cache_k, cache_v, cache_kidx, page_table = cache
        n_pages = page_table.shape[1]
        k_top = min(TOPK_MAX, (n_pages * PAGE_SIZE + seq_len) // 4)
        n_pool = cache_k.shape[0]
        qi = z[:, off["qi"]:off["qi"] + IDX_HEADS * IDX_DIM]
        qall = qi.reshape(n_seq, seq_len, IDX_HEADS, IDX_DIM).transpose(0, 2, 1, 3).reshape(
            n_seq, IDX_HEADS * seq_len, IDX_DIM)
        wi = z[:, off["wi"]:off["wi"] + IDX_HEADS]
        wcol = wi.reshape(n_seq, seq_len, IDX_HEADS).transpose(0, 2, 1).reshape(n_seq, IDX_HEADS * seq_len, 1)
        group = d // ATT_HEAD_DIM // KV_HEADS
        qg = qn.reshape(n_seq, seq_len, KV_HEADS, group, ATT_HEAD_DIM).transpose(0, 2, 3, 1, 4).reshape(
            n_seq, KV_HEADS, group * seq_len, ATT_HEAD_DIM)
        o_g = _dsa_sample(page_table, qall, wcol, qg, ki_new.reshape(n_seq, seq_len, IDX_DIM),
                          k_new.reshape(n_seq, seq_len, kv), v_new.reshape(n_seq, seq_len, kv), cache_kidx,
                          cache_k.reshape(n_pool, PAGE_SIZE, kv), cache_v.reshape(n_pool, PAGE_SIZE, kv), k_top)
        o_att = o_g.reshape(n_seq, KV_HEADS, group, seq_len, ATT_HEAD_DIM).transpose(0, 3, 1, 2, 4).reshape(
            m, d).astype(BF16)

    mixed = _mix(o_rw, o_att, W["w_proj_a"], W["w_proj_b"], z, off, d)
    x1 = _matmul(mixed, W["w_out"], tm=512, tn=512, res=xf, gate=mod, gate_col=2, seq_len=seq_len, name="out_proj")

    h2 = _norm_mod(x1, W["norm2_g"], mod, 4, 3, seq_len)
    u = _matmul(h2, W["w_ffn_up"], tm=1024, tn=1024, name="ffn_up")
    d_ff = W["w_ffn_down"].shape[0]
    act = _conv_ffn(u, conv_prev, W["conv_w"], W["conv_b"], seq_len, d_ff)
    x2 = _matmul(act, W["w_ffn_down"], tm=512, tn=512, res=x1, gate=mod, gate_col=5, seq_len=seq_len, name="ffn_down")
    ug = u[:, :d_ff].reshape(n_seq, seq_len, d_ff)
    conv_new = jnp.concatenate([conv_prev.astype(u.dtype), ug], axis=1)[:, -(CONV_W - 1):]

    hd = lambda a, w: a.reshape(n_seq, seq_len, KV_HEADS, w)
    return (x2.reshape(n_seq, seq_len, d), hd(k_new, ATT_HEAD_DIM), hd(v_new, ATT_HEAD_DIM),
            ki_new.reshape(n_seq, seq_len, IDX_DIM), s_new, shift_new, conv_new)


def kernel(x_prompt, x_sample, c_prompt, c_sample, cache_k, cache_v, cache_kidx, page_table, state_rwkv,
           state_shift, state_conv, w_ada, b_ada, norm1_g, norm2_g, w_in, shift_mu, w0, w_decay_up, a0, w_a_up,
           w_g_up, k_k, k_a, r_k, lnx_w, lnx_b, q_norm_g, k_norm_g, w_proj_a, w_proj_b, w_out, w_ffn_up, conv_w,
           conv_b, w_ffn_down):
    depth = w_in.shape[0]
    d = x_prompt.shape[-1]
    bp, bs = x_prompt.shape[0], x_sample.shape[0]
    off, _ = _in_layout(d)
    n_rw = d // RW_HEAD
    yp, ys = x_prompt, x_sample
    st_p, st_s = [], []
    pad_rows = lambda a, rows: jnp.concatenate([a, jnp.zeros((rows - a.shape[0],) + a.shape[1:], a.dtype)], axis=0)
    for l in range(depth):
        heads = lambda a: a.reshape(n_rw, 1, RW_HEAD)
        mu = shift_mu[l]
        mu_lora = jnp.concatenate(
            [mu[3 * d:3 * d + DECAY_LORA], jnp.zeros((LANES - DECAY_LORA,), F32),
             mu[3 * d + DECAY_LORA:3 * d + DECAY_LORA + ICLR_LORA], jnp.zeros((LANES - ICLR_LORA,), F32),
             mu[3 * d + DECAY_LORA + ICLR_LORA:]])
        W = {
            "off": off,
            "norm1_g": norm1_g[l][None], "norm2_g": norm2_g[l][None],
            "w_in": _pack_cols(w_in[l], d).astype(BF16),
            "mu_packed": jnp.concatenate([mu[:3 * d], mu_lora])[None],
            "w0": w0[l][None], "a0": a0[l][None],
            "w_decay_up": pad_rows(w_decay_up[l], LANES).astype(BF16),
            "w_a_up": pad_rows(w_a_up[l], LANES).astype(BF16),
            "w_g_up": w_g_up[l].astype(BF16),
            "rw_params": [heads(k_k[l]), heads(k_a[l]), heads(r_k[l]), heads(lnx_w[l]), heads(lnx_b[l])],
            "rw_rows": [a.reshape(1, d) for a in (k_k[l], k_a[l], r_k[l], lnx_w[l], lnx_b[l])],
            "q_norm_g": q_norm_g[l][None], "k_norm_g": k_norm_g[l][None],
            "w_proj_a": w_proj_a[l].astype(BF16), "w_proj_b": w_proj_b[l].astype(BF16),
            "w_out": w_out[l].astype(BF16), "w_ffn_up": w_ffn_up[l].astype(BF16),
            "conv_w": conv_w[l], "conv_b": conv_b[l][None], "w_ffn_down": w_ffn_down[l].astype(BF16),
        }
        c_all = jnp.concatenate([c_prompt, c_sample], axis=0)
        rows = -(-c_all.shape[0] // SUBLANES) * SUBLANES
        mod = _matmul(pad_rows(c_all, rows), w_ada[l].astype(BF16), tm=rows, tn=1024, bias=b_ada[l][None],
                      pre_silu=True, name="ada_mod")
        mod_p, mod_s = mod[:bp], mod[bp:bp + bs]
        d_ff = w_ffn_down.shape[1]
        yp, *sp = _layer(yp, mod_p, jnp.zeros((bp, state_shift.shape[-1]), F32),
                         jnp.zeros((bp, n_rw, RW_HEAD, RW_HEAD), F32), jnp.zeros((bp, CONV_W - 1, d_ff), F32),
                         W, None)
        ys, *ss = _layer(ys, mod_s, state_shift[l], state_rwkv[l], state_conv[l], W,
                         (cache_k[l], cache_v[l], cache_kidx[l], page_table))
        st_p.append(sp)
        st_s.append(ss)
    stk = lambda sts, i: jnp.stack([s[i] for s in sts], axis=0)
    return (yp, ys,
            stk(st_p, 0), stk(st_p, 1), stk(st_p, 2), stk(st_p, 3), stk(st_p, 4), stk(st_p, 5),
            stk(st_s, 0), stk(st_s, 1), stk(st_s, 2), stk(st_s, 3), stk(st_s, 4), stk(st_s, 5))
```

```python
import functools

import numpy as np
import jax
import jax.numpy as jnp
from jax import lax
from jax.experimental import pallas as pl
from jax.experimental.pallas import tpu as pltpu

F32 = jnp.float32
BF16 = jnp.bfloat16
I32 = jnp.int32

NORM_EPS = 1e-6
LNX_EPS = 64e-5
RW_HEAD = 64
ATT_HEAD_DIM = 128
KV_HEADS = 4
IDX_HEADS = 16
IDX_DIM = 64
PAGE_SIZE = 128
TOPK_MAX = 256
CONV_W = 3
DECAY_LORA = 96
ICLR_LORA = 96
GATE_LORA = 256

LANES = 128
SUBLANES = 8
VMEM_LIMIT = 56 * 1024 * 1024
NEG_INF = float("-inf")
INT_MIN = -2 ** 31


def _cparams(n_axes):
    return pltpu.CompilerParams(dimension_semantics=("arbitrary",) * n_axes,
                                vmem_limit_bytes=VMEM_LIMIT)


def _expand_rows(v, tm):
    s, w = v.shape
    if s == 1:
        return v
    return jnp.broadcast_to(v[:, None, :], (s, tm // s, w)).reshape(tm, w)


def _sigmoid(x):
    return 1.0 / (1.0 + jnp.exp(-x))


def _in_layout(d):
    rw = d
    att = d
    kv = KV_HEADS * ATT_HEAD_DIM
    qi = IDX_HEADS * IDX_DIM
    off = {}
    pos = 0
    for name, width in (("ga", d), ("gb", d), ("r", rw), ("k", rw), ("v", rw), ("q", att), ("qi", qi),
                        ("kk", kv), ("vv", kv), ("lora", 512), ("ki2", LANES), ("wi", LANES)):
        assert pos % width == 0, (name, pos, width)
        off[name] = pos
        pos += width
    total = -(-pos // 1024) * 1024
    return off, total


def _pack_cols(w, d):
    rw, att = d, d
    kv = KV_HEADS * ATT_HEAD_DIM
    qi = IDX_HEADS * IDX_DIM
    sizes = (rw, rw, rw, DECAY_LORA, ICLR_LORA, GATE_LORA, att, kv, kv, qi, IDX_DIM, IDX_HEADS, d, d)
    cuts = np.cumsum((0,) + sizes)
    seg = {n: w[..., cuts[i]:cuts[i + 1]] for i, n in enumerate(
        ("r", "k", "v", "wd", "ad", "gd", "q", "kk", "vv", "qi", "ki", "wi", "ga", "gb"))}
    z = lambda n: jnp.zeros(w.shape[:-1] + (n,), w.dtype)
    _, total = _in_layout(d)
    parts = [seg["ga"], seg["gb"], seg["r"], seg["k"], seg["v"], seg["q"], seg["qi"], seg["kk"], seg["vv"],
             seg["wd"], z(LANES - DECAY_LORA), seg["ad"], z(LANES - ICLR_LORA), seg["gd"],
             seg["ki"], seg["ki"], seg["wi"], z(LANES - IDX_HEADS)]
    used = sum(p.shape[-1] for p in parts)
    parts.append(z(total - used))
    return jnp.concatenate(parts, axis=-1)


def _mm_kernel(*refs, pre_silu, has_bias, has_res):
    x_ref, w_ref = refs[0], refs[1]
    o_ref = refs[-1]
    x = x_ref[...]
    if pre_silu:
        x = x.astype(F32)
        x = x * _sigmoid(x)
    acc = jnp.dot(x.astype(BF16), w_ref[...], preferred_element_type=F32)
    nxt = 2
    if has_bias:
        acc = acc + refs[nxt][...]
        nxt += 1
    if has_res:
        res_ref, gate_ref = refs[nxt], refs[nxt + 1]
        acc = res_ref[...] + _expand_rows(gate_ref[...], acc.shape[0]) * acc
    o_ref[...] = acc.astype(o_ref.dtype)


def _matmul(x, w, *, tm, tn, bias=None, pre_silu=False, res=None, gate=None, gate_col=0, seq_len=None,
            out_dtype=F32, name="mm"):
    m, k = x.shape
    n = w.shape[1]
    tm = min(tm, m)
    assert m % tm == 0 and n % tn == 0, (m, tm, n, tn)
    in_specs = [pl.BlockSpec((tm, k), lambda i, j: (i, 0)),
                pl.BlockSpec((k, tn), lambda i, j: (0, j))]
    args = [x, w]
    if bias is not None:
        in_specs.append(pl.BlockSpec((1, tn), lambda i, j: (0, j)))
        args.append(bias)
    if res is not None:
        in_specs.append(pl.BlockSpec((tm, tn), lambda i, j: (i, j)))
        args.append(res)
        gcol = gate_col * (n // tn)
        if seq_len >= tm:
            in_specs.append(pl.BlockSpec((1, tn), lambda i, j: ((i * tm) // seq_len, gcol + j)))
        else:
            in_specs.append(pl.BlockSpec((tm // seq_len, tn), lambda i, j: (i, gcol + j)))
        args.append(gate)
    return pl.pallas_call(
        functools.partial(_mm_kernel, pre_silu=pre_silu, has_bias=bias is not None, has_res=res is not None),
        grid=(m // tm, n // tn),
        in_specs=in_specs,
        out_specs=pl.BlockSpec((tm, tn), lambda i, j: (i, j)),
        out_shape=jax.ShapeDtypeStruct((m, n), out_dtype),
        compiler_params=_cparams(2),
        name=name,
    )(*args)


def _norm_mod_kernel(x_ref, g_ref, sc_ref, sh_ref, o_ref):
    x = x_ref[...]
    tm = x.shape[0]
    y = x * lax.rsqrt(jnp.mean(x * x, axis=-1, keepdims=True) + NORM_EPS)
    y = y * g_ref[...]
    y = y * (1.0 + _expand_rows(sc_ref[...], tm)) + _expand_rows(sh_ref[...], tm)
    o_ref[...] = y.astype(o_ref.dtype)


def _mod_spec(tm, width, col, seq_len):
    if seq_len >= tm:
        return pl.BlockSpec((1, width), lambda i: ((i * tm) // seq_len, col))
    return pl.BlockSpec((tm // seq_len, width), lambda i: (i, col))


def _norm_mod(x, g, mod, sc_col, sh_col, seq_len, tm=256):
    m, d = x.shape
    tm = min(tm, m)
    return pl.pallas_call(
        _norm_mod_kernel,
        grid=(m // tm,),
        in_specs=[pl.BlockSpec((tm, d), lambda i: (i, 0)),
                  pl.BlockSpec((1, d), lambda i: (0, 0)),
                  _mod_spec(tm, d, sc_col, seq_len),
                  _mod_spec(tm, d, sh_col, seq_len)],
        out_specs=pl.BlockSpec((tm, d), lambda i: (i, 0)),
        out_shape=jax.ShapeDtypeStruct((m, d), BF16),
        compiler_params=_cparams(1),
        name="norm_mod",
    )(x, g, mod, mod)


def _prev_rows(x, halo_ref, state_refs, i, seq_len, shift):
    tm = x.shape[0]
    row = lax.broadcasted_iota(I32, (tm, 1), 0)
    out = pltpu.roll(x, shift, 0)
    n_state = len(state_refs)
    for r in range(shift):
        back = shift - r
        state_row = state_refs[n_state - back][...]
        if seq_len >= tm:
            src = jnp.where(i == 0, state_row, halo_ref[SUBLANES - back:SUBLANES - back + 1, :])
            out = jnp.where(row == r, src, out)
        else:
            out = jnp.where(row % seq_len == r, _expand_rows(state_row, tm), out)
    return out


def _halo_spec(tm, width, col):
    return pl.BlockSpec((SUBLANES, width), lambda i: (jnp.maximum(i * (tm // SUBLANES) - 1, 0), col))


def _state_spec(tm, width, col, seq_len):
    if seq_len >= tm:
        return pl.BlockSpec((1, width), lambda i: ((i * tm) // seq_len, col))
    return pl.BlockSpec((tm // seq_len, width), lambda i: (i, col))


def _softplus(x):
    return jnp.maximum(x, 0.0) + jnp.log1p(jnp.exp(-jnp.abs(x)))


def _rwkv_prep_kernel(zr_ref, zk_ref, zv_ref, zl_ref, hr_ref, hk_ref, hv_ref, hl_ref,
                      sr_ref, sk_ref, sv_ref, sl_ref, mur_ref, muk_ref, muv_ref, mul_ref,
                      w0_ref, wdec_ref, a0_ref, wa_ref, wg_ref,
                      r_ref, k_ref, v_ref, dec_ref, ic_ref, g_ref, *, seq_len):
    i = pl.program_id(0)

    def shifted(z_ref, h_ref, s_ref, mu_ref):
        x = z_ref[...]
        prev = _prev_rows(x, h_ref, [s_ref], i, seq_len, 1)
        return x + (prev - x) * mu_ref[...]

    r_ref[...] = shifted(zr_ref, hr_ref, sr_ref, mur_ref)
    k_ref[...] = shifted(zk_ref, hk_ref, sk_ref, muk_ref)
    v_ref[...] = shifted(zv_ref, hv_ref, sv_ref, muv_ref)
    zl = shifted(zl_ref, hl_ref, sl_ref, mul_ref)
    wd = zl[:, 0:LANES]
    ad = zl[:, LANES:2 * LANES]
    gd = zl[:, 2 * LANES:4 * LANES]
    mm = lambda a, w_ref: jnp.dot(a.astype(BF16), w_ref[...], preferred_element_type=F32)
    dec_in = w0_ref[...] + mm(jnp.tanh(wd), wdec_ref)
    w_log = -_softplus(-dec_in) - 0.5
    dec_ref[...] = -jnp.exp(w_log)
    ic_ref[...] = _sigmoid(a0_ref[...] + mm(ad, wa_ref))
    g_ref[...] = mm(_sigmoid(gd), wg_ref)


def _rwkv_prep(z, off, state_packed, mu_packed, w0, wdec, a0, wa, wg, seq_len, d, tm=256):
    m = z.shape[0]
    tm = min(tm, m)
    cb = lambda name, width: off[name] // width
    zspec = lambda name, width: pl.BlockSpec((tm, width), lambda i: (i, cb(name, width)))
    sspec = lambda c, width: _state_spec(tm, width, c, seq_len)
    mspec = lambda c, width: pl.BlockSpec((1, width), lambda i: (0, c))
    full = lambda a: pl.BlockSpec(a.shape, lambda i: (0,) * a.ndim)
    lcol = 3 * d // 512
    out_spec = pl.BlockSpec((tm, d), lambda i: (i, 0))
    out_sds = jax.ShapeDtypeStruct((m, d), F32)
    return pl.pallas_call(
        functools.partial(_rwkv_prep_kernel, seq_len=seq_len),
        grid=(m // tm,),
        in_specs=[zspec("r", d), zspec("k", d), zspec("v", d), zspec("lora", 512),
                  _halo_spec(tm, d, cb("r", d)), _halo_spec(tm, d, cb("k", d)), _halo_spec(tm, d, cb("v", d)),
                  _halo_spec(tm, 512, cb("lora", 512)),
                  sspec(0, d), sspec(1, d), sspec(2, d), sspec(lcol, 512),
                  mspec(0, d), mspec(1, d), mspec(2, d), mspec(lcol, 512),
                  full(w0), full(wdec), full(a0), full(wa), full(wg)],
        out_specs=[out_spec] * 6,
        out_shape=[out_sds] * 6,
        compiler_params=_cparams(1),
        name="rwkv_prep",
    )(z, z, z, z, z, z, z, z, state_packed, state_packed, state_packed, state_packed,
      mu_packed, mu_packed, mu_packed, mu_packed, w0, wdec, a0, wa, wg)


def _scan_kernel(r_ref, logw_ref, k_ref, v_ref, ic_ref, g_ref, s0_ref, kk_ref, ka_ref, rk_ref, lw_ref, lb_ref,
                 o_ref, sout_ref, s_scr, a_scr, b_scr, km_scr, o_scr, w_ref, *, gsub):
    ti = pl.program_id(1)
    gb, tb, n = r_ref.shape

    @pl.when(ti == 0)
    def _():
        s_scr[...] = s0_ref[...]

    for g in range(gb):
        kraw = k_ref[g]
        ic = ic_ref[g]
        kk = kraw * kk_ref[g]
        nrm = jnp.sqrt(jnp.sum(kk * kk, axis=-1, keepdims=True))
        kk = kk / jnp.maximum(nrm, 1e-12)
        a_scr[g] = -kk
        b_scr[g] = kk * ic
        km_scr[g] = kraw * (1.0 + (ic - 1.0) * ka_ref[g])
        w_ref[g] = jnp.exp(logw_ref[g])

    eye =(lax.broadcasted_iota(I32, (n, n), 0) == lax.broadcasted_iota(I32, (n, n), 1)).astype(F32)

    for g0 in range(0, gb, gsub):
        heads = range(g0, min(g0 + gsub, gb))

        def step(t, carry):
            new = []
            for s, g in zip(carry, heads):
                row = lambda ref: ref[g, pl.ds(t, 1), :]
                sa = jnp.sum(s * row(a_scr), axis=1, keepdims=True)
                vcol = jnp.sum(eye * row(v_ref), axis=1, keepdims=True)
                s = s * row(w_ref) + sa * row(b_scr) + vcol * row(km_scr)
                ocol = jnp.sum(s * row(r_ref), axis=1, keepdims=True)
                o_scr[g, pl.ds(t, 1), :] = jnp.sum(eye * ocol, axis=0, keepdims=True)
                new.append(s)
            return tuple(new)

        fin = lax.fori_loop(0, tb, step, tuple(s_scr[g] for g in heads))
        for s, g in zip(fin, heads):
            s_scr[g] = s

    for g in range(gb):
        o = o_scr[g]
        mu = jnp.mean(o, axis=-1, keepdims=True)
        var = jnp.mean(jnp.square(o - mu), axis=-1, keepdims=True)
        on = (o - mu) * lax.rsqrt(var + LNX_EPS) * lw_ref[g] + lb_ref[g]
        bonus = jnp.sum(r_ref[g] * km_scr[g] * rk_ref[g], axis=-1, keepdims=True) * v_ref[g]
        o_ref[g] = (on + bonus) * g_ref[g]

    sout_ref[...] = s_scr[...]


def _rwkv_scan(seqs, s0, params, gb, tb, gsub=4):
    g_tot, t_tot, n = seqs[0].shape
    n_heads = params[0].shape[0]
    assert g_tot % gb == 0 and t_tot % tb == 0 and n_heads % gb == 0
    hblocks = n_heads // gb
    seq_spec = pl.BlockSpec((gb, tb, n), lambda gi, ti: (gi, ti, 0))
    st_spec = pl.BlockSpec((gb, n, n), lambda gi, ti: (gi, 0, 0))
    par_spec = pl.BlockSpec((gb, 1, n), lambda gi, ti: (gi % hblocks, 0, 0))
    return pl.pallas_call(
        functools.partial(_scan_kernel, gsub=gsub),
        grid=(g_tot // gb, t_tot // tb),
        in_specs=[seq_spec] * 6 + [st_spec] + [par_spec] * 5,
        out_specs=[seq_spec, st_spec],
        out_shape=[jax.ShapeDtypeStruct((g_tot, t_tot, n), F32), jax.ShapeDtypeStruct((g_tot, n, n), F32)],
        scratch_shapes=[pltpu.VMEM((gb, n, n), F32)] + [pltpu.VMEM((gb, tb, n), F32)] * 5,
        compiler_params=_cparams(2),
        name="rwkv_scan",
    )(*seqs, s0, *params)


SCAN_CHUNK = 64


def _split_bf16(x):
    hi = x.astype(BF16)
    return hi, (x - hi.astype(F32)).astype(BF16)


def _dot3(a, b, contract):
    dg = lambda x, y: lax.dot_general(x, y, (contract, ((), ())), preferred_element_type=F32)
    return dg(a[0], b[0]) + (dg(a[0], b[1]) + dg(a[1], b[0]))


_NN = ((1,), (0,))
_NT = ((1,), (1,))
_TN = ((0,), (0,))


def _scan_chunk_kernel(r_ref, logw_ref, k_ref, v_ref, ic_ref, g_ref, kk_ref, ka_ref, rk_ref, lw_ref, lb_ref,
                       o_ref, sout_ref, sb_scr, o_scr, km_scr):
    ti = pl.program_id(1)
    tb, width = r_ref.shape
    n_pairs = width // LANES
    n_chunks = tb // SCAN_CHUNK
    c = SCAN_CHUNK
    n = RW_HEAD

    @pl.when(ti == 0)
    def _():
        sb_scr[...] = jnp.zeros_like(sb_scr)

    lane = lax.broadcasted_iota(I32, (1, LANES), 1)
    head0 = lane < n
    rowc = lax.broadcasted_iota(I32, (c, 1), 0)
    bf = lambda x: x.astype(BF16)
    dg = lambda x, y, contract: lax.dot_general(x, y, (contract, ((), ())), preferred_element_type=F32)

    def hsum(x):
        s0 = jnp.sum(jnp.where(head0, x, 0.0), axis=1, keepdims=True)
        s1 = jnp.sum(jnp.where(head0, 0.0, x), axis=1, keepdims=True)
        return jnp.where(head0, s0, s1)

    def stack(x):
        return jnp.concatenate([jnp.where(head0, x, 0.0), jnp.where(head0, 0.0, x)], axis=0)

    rl = lax.broadcasted_iota(I32, (2 * c, 2 * c), 0) % c
    cl = lax.broadcasted_iota(I32, (2 * c, 2 * c), 1) % c
    strict = cl < rl
    incl = cl <= rl

    def prepare(ci, pi):
        sl = slice(ci * c, (ci + 1) * c)
        cols = slice(pi * LANES, (pi + 1) * LANES)
        r, logw, kraw, v, ic = (ref[sl, cols] for ref in (r_ref, logw_ref, k_ref, v_ref, ic_ref))
        kk = kraw * kk_ref[:, cols]
        kk = kk / jnp.maximum(jnp.sqrt(hsum(kk * kk)), 1e-12)
        kmod = kraw * (1.0 + (ic - 1.0) * ka_ref[:, cols])
        km_scr[sl, cols] = kmod
        cum = logw
        sh = 1
        while sh < c:
            cum = cum + jnp.where(rowc >= sh, pltpu.roll(cum, sh, 0), 0.0)
            sh *= 2
        p_inv = jnp.exp(-cum)
        a_t = -kk * jnp.exp(cum - logw)
        r_t = r * jnp.exp(cum)
        b_t = kk * ic * p_inv
        k_t = kmod * p_inv
        ar = _split_bf16(jnp.concatenate([stack(a_t), stack(r_t)], axis=0))
        bk = _split_bf16(jnp.concatenate([stack(b_t), stack(k_t)], axis=0))
        vst = stack(v)
        g4 = dg(ar[0], bk[0], _NT)
        a_ab = jnp.where(strict, g4[0:2 * c, 0:2 * c], 0.0)
        a_ak = jnp.where(strict, g4[0:2 * c, 2 * c:4 * c], 0.0)
        a_rb = jnp.where(incl, g4[2 * c:4 * c, 0:2 * c], 0.0)
        a_rk = jnp.where(incl, g4[2 * c:4 * c, 2 * c:4 * c], 0.0)
        vs = _split_bf16(vst)
        return dict(ar=ar, bk=bk, vs=vs, mpow=a_ab, tm1=a_ab, a_rb=bf(a_rb), akv=dg(bf(a_ak), vs[0], _NN),
                    rkv=dg(bf(a_rk), vs[0], _NN), p_end=jnp.exp(cum[c - 1:c, :]))

    pre = [[prepare(ci, pi) for pi in range(n_pairs)] for ci in range(n_chunks)]
    items = [p for row in pre for p in row]
    span = 2
    while span < c:
        for p in items:
            mb = bf(p["mpow"])
            p["mpow"] = dg(mb, mb, _NN)
        for p in items:
            p["tm1"] = p["tm1"] + p["mpow"] + dg(bf(p["tm1"]), bf(p["mpow"]), _NN)
        span *= 2
    for p in items:
        p["tm1"] = bf(p["tm1"])

    sbs = [sb_scr[pi] for pi in range(n_pairs)]
    for ci in range(n_chunks):
        for pi in range(n_pairs):
            p = pre[ci][pi]
            xr = _dot3(p["ar"], _split_bf16(sbs[pi]), _NT)
            y = xr[0:2 * c] + p["akv"]
            ust = y + dg(p["tm1"], bf(y), _NN)
            us = _split_bf16(ust)
            ost = xr[2 * c:4 * c] + dg(p["a_rb"], us[0], _NN) + p["rkv"]
            o_scr[ci * c:(ci + 1) * c, pi * LANES:(pi + 1) * LANES] = ost[0:c] + ost[c:2 * c]
            uv = (jnp.concatenate([us[0], p["vs"][0]], axis=0), jnp.concatenate([us[1], p["vs"][1]], axis=0))
            sbs[pi] = (sbs[pi] + _dot3(uv, p["bk"], _TN)) * p["p_end"]
    for pi in range(n_pairs):
        sb_scr[pi] = sbs[pi]
        sout_ref[pi] = sbs[pi]

    for pi in range(n_pairs):
        cols = slice(pi * LANES, (pi + 1) * LANES)
        o = o_scr[:, cols]
        mu = hsum(o) * (1.0 / n)
        var = hsum(jnp.square(o - mu)) * (1.0 / n)
        on = (o - mu) * lax.rsqrt(var + LNX_EPS) * lw_ref[:, cols] + lb_ref[:, cols]
        bonus = hsum(r_ref[:, cols] * km_scr[:, cols] * rk_ref[:, cols]) * v_ref[:, cols]
        o_ref[:, cols] = ((on + bonus) * g_ref[:, cols]).astype(o_ref.dtype)


SCAN_PAIRS = 2


def _rwkv_scan_chunked(seqs, params, tb=512):
    t_tot, d = seqs[0].shape
    tb = min(tb, t_tot)
    assert t_tot % tb == 0 and tb % SCAN_CHUNK == 0 and d % LANES == 0
    n_pairs = d // LANES
    width = SCAN_PAIRS * LANES
    assert d % width == 0
    seq_spec = pl.BlockSpec((tb, width), lambda hp, ti: (ti, hp))
    par_spec = pl.BlockSpec((1, width), lambda hp, ti: (0, hp))
    return pl.pallas_call(
        _scan_chunk_kernel,
        grid=(d // width, t_tot // tb),
        in_specs=[seq_spec] * 6 + [par_spec] * 5,
        out_specs=[seq_spec, pl.BlockSpec((SCAN_PAIRS, LANES, LANES), lambda hp, ti: (hp, 0, 0))],
        out_shape=[jax.ShapeDtypeStruct((t_tot, d), BF16), jax.ShapeDtypeStruct((n_pairs, LANES, LANES), F32)],
        scratch_shapes=[pltpu.VMEM((SCAN_PAIRS, LANES, LANES), F32), pltpu.VMEM((tb, width), F32),
                        pltpu.VMEM((tb, width), F32)],
        compiler_params=_cparams(2),
        name="rwkv_scan_chunked",
    )(*seqs, *params)


def _dsa_prep_kernel(q_ref, k_ref, qg_ref, kg_ref, qo_ref, ko_ref, kb_ref):
    def head_norm(x, g):
        outs = []
        for h in range(x.shape[1] // ATT_HEAD_DIM):
            xh = x[:, h * ATT_HEAD_DIM:(h + 1) * ATT_HEAD_DIM]
            outs.append(xh * lax.rsqrt(jnp.mean(xh * xh, axis=-1, keepdims=True) + NORM_EPS) * g)
        return jnp.concatenate(outs, axis=1)

    qo_ref[...] = head_norm(q_ref[...], qg_ref[...]).astype(qo_ref.dtype)
    kn = head_norm(k_ref[...], kg_ref[...])
    ko_ref[...] = kn
    kb_ref[...] = kn.astype(kb_ref.dtype)


def _dsa_prep(z, off, qg, kg, d, tm=256):
    m = z.shape[0]
    tm = min(tm, m)
    kv = KV_HEADS * ATT_HEAD_DIM
    return pl.pallas_call(
        _dsa_prep_kernel,
        grid=(m // tm,),
        in_specs=[pl.BlockSpec((tm, d), lambda i: (i, off["q"] // d)),
                  pl.BlockSpec((tm, kv), lambda i: (i, off["kk"] // kv)),
                  pl.BlockSpec((1, ATT_HEAD_DIM), lambda i: (0, 0)),
                  pl.BlockSpec((1, ATT_HEAD_DIM), lambda i: (0, 0))],
        out_specs=[pl.BlockSpec((tm, d), lambda i: (i, 0)),
                   pl.BlockSpec((tm, kv), lambda i: (i, 0)),
                   pl.BlockSpec((tm, kv), lambda i: (i, 0))],
        out_shape=[jax.ShapeDtypeStruct((m, d), BF16), jax.ShapeDtypeStruct((m, kv), F32),
                   jax.ShapeDtypeStruct((m, kv), BF16)],
        compiler_params=_cparams(1),
        name="dsa_prep",
    )(z, z, qg, kg)


def _sort_key(x):
    x = jnp.where(x == 0.0, 0.0, x)
    bits = pltpu.bitcast(x, I32)
    return bits ^ (lax.shift_right_arithmetic(bits, 31) & 0x7FFFFFFF)


def _topk_threshold(skey_ref, n_blocks, k_top, rows):
    def count(pred_fn):
        hit = lambda j: jnp.where(pred_fn(skey_ref[j]), 1.0, 0.0)
        if isinstance(n_blocks, int):
            acc = hit(0)
            for j in range(1, n_blocks):
                acc = acc + hit(j)
        else:
            body = lambda j2, acc: acc + (hit(2 * j2) + hit(2 * j2 + 1))
            acc = lax.fori_loop(0, (n_blocks + 1) // 2, body, jnp.zeros((rows, LANES), F32))
        return jnp.sum(acc, axis=1, keepdims=True)

    def bit_body(it, tau):
        cand = tau + lax.shift_left(jnp.int32(1), 31 - it)
        return jnp.where(count(lambda s: s >= cand) >= k_top, cand, tau)

    tau = lax.fori_loop(0, 32, bit_body, jnp.full((rows, 1), INT_MIN, I32))
    need = k_top - count(lambda s: s > tau)
    return tau, need


def _selected(skey, tau, need, eq_before, tri):
    eq = skey == tau
    eqf = jnp.where(eq, 1.0, 0.0)
    prefix = eq_before + jnp.dot(eqf.astype(BF16), tri, preferred_element_type=F32)
    sel = (skey > tau) | (eq & (prefix <= need))
    return sel, eq_before + jnp.sum(eqf, axis=1, keepdims=True)


def _tri_incl():
    return (lax.broadcasted_iota(I32, (LANES, LANES), 0) <= lax.broadcasted_iota(I32, (LANES, LANES), 1)).astype(BF16)


def _sel_prompt_kernel(qi_ref, wi_ref, ki_ref, bias_ref, skey_scr, qm_scr, wb_scr, *, k_top):
    i = pl.program_id(0)
    qb = qi_ref.shape[0]
    nkb = bias_ref.shape[1] // LANES
    kcols = lambda j: pl.ds(pl.multiple_of(j * LANES, LANES), LANES)
    lane = lax.broadcasted_iota(I32, (qb, LANES), 1)
    rowq = lax.broadcasted_iota(I32, (qb, LANES), 0)
    wi = wi_ref[...] * (IDX_HEADS ** -0.5)
    skey_scr[jnp.minimum(i + 1, nkb - 1)] = _sort_key(jnp.full((qb, LANES), NEG_INF, F32))
    for hp in range(IDX_HEADS // 2):
        qpair = qi_ref[:, hp * LANES:(hp + 1) * LANES] * (IDX_DIM ** -0.5)
        qm_scr[2 * hp] = jnp.where(lane < IDX_DIM, qpair, 0.0).astype(BF16)
        qm_scr[2 * hp + 1] = jnp.where(lane >= IDX_DIM, qpair, 0.0).astype(BF16)
    for h in range(IDX_HEADS):
        wb_scr[h] = jnp.broadcast_to(wi[:, h:h + 1], (qb, LANES))

    def score_block(j, carry):
        kblk = ki_ref[pl.ds(pl.multiple_of(j * LANES, LANES), LANES), :].astype(BF16)
        sc = jnp.zeros((qb, LANES), F32)
        for h in range(IDX_HEADS):
            dots = lax.dot_general(qm_scr[h], kblk, (((1,), (1,)), ((), ())), preferred_element_type=F32)
            sc = sc + jnp.maximum(dots, 0.0) * wb_scr[h]
        allowed = (j < i) | (lane <= rowq)
        skey_scr[j] = _sort_key(jnp.where(allowed, sc, NEG_INF))
        return carry

    lax.fori_loop(0, i + 1, score_block, 0)
    tau, need = _topk_threshold(skey_scr, i + 1, float(k_top), qb)
    tri = _tri_incl()

    def write_block(j, eq_before):
        sel, eq_after = _selected(skey_scr[j], tau, need, eq_before, tri)
        allowed = (j < i) | (lane <= rowq)
        bias_ref[:, kcols(j)] = jnp.where(sel & allowed, 0.0, NEG_INF).astype(bias_ref.dtype)
        return eq_after

    lax.fori_loop(0, i + 1, write_block, jnp.zeros((qb, 1), F32))

    def fill_block(j, carry):
        bias_ref[:, kcols(j)] = jnp.full((qb, LANES), NEG_INF, bias_ref.dtype)
        return carry

    lax.fori_loop(i + 1, nkb, fill_block, 0)


def _sel_prompt(z, off, k_top):
    t = z.shape[0]
    qb = LANES
    nkb = t // LANES
    nqb = t // qb
    qi_w = IDX_HEADS * IDX_DIM
    return pl.pallas_call(
        functools.partial(_sel_prompt_kernel, k_top=k_top),
        grid=(nqb,),
        in_specs=[pl.BlockSpec((qb, qi_w), lambda i: (i, off["qi"] // qi_w)),
                  pl.BlockSpec((qb, LANES), lambda i: (i, off["wi"] // LANES)),
                  pl.BlockSpec((t, LANES), lambda i: (0, off["ki2"] // LANES))],
        out_specs=pl.BlockSpec((qb, t), lambda i: (i, 0)),
        out_shape=jax.ShapeDtypeStruct((t, t), BF16),
        scratch_shapes=[pltpu.VMEM((nkb, qb, LANES), I32),
                        pltpu.VMEM((IDX_HEADS, qb, LANES), BF16),
                        pltpu.VMEM((IDX_HEADS, qb, LANES), F32)],
        compiler_params=_cparams(1),
        name="dsa_select_prompt",
    )(z, z, z)


ATT_TILE = 256


def _att_prompt_kernel(qidx_ref, kidx_ref, q_ref, k_ref, v_ref, bias_ref, o_ref, m_scr, l_scr, acc_scr):
    p = pl.program_id(0)
    qi = qidx_ref[p]
    kj = kidx_ref[p]
    tq = q_ref.shape[0]
    n_heads = q_ref.shape[1] // ATT_HEAD_DIM
    group = n_heads // KV_HEADS
    first = kj == 0
    bias = bias_ref[...].astype(F32)
    bias = jnp.concatenate([bias] * group, axis=0)
    for n in range(KV_HEADS):
        cols = lambda h: slice(h * ATT_HEAD_DIM, (h + 1) * ATT_HEAD_DIM)
        q = jnp.concatenate([q_ref[:, cols(n * group + g)] for g in range(group)], axis=0)
        s = lax.dot_general(q, k_ref[:, cols(n)], (_NT, ((), ())), preferred_element_type=F32)
        s = s * (ATT_HEAD_DIM ** -0.5) + bias
        m_prev = jnp.where(first, NEG_INF, m_scr[n])
        l_prev = jnp.where(first, 0.0, l_scr[n])
        acc_prev = jnp.where(first, 0.0, acc_scr[n])
        m_cur = jnp.maximum(m_prev, jnp.max(s, axis=1, keepdims=True))
        m_safe = jnp.where(m_cur == NEG_INF, 0.0, m_cur)
        pexp = jnp.exp(s - jnp.concatenate([m_safe] * (s.shape[1] // LANES), axis=1))
        alpha = jnp.exp(m_prev - m_safe)
        l_scr[n] = alpha * l_prev + jnp.sum(pexp, axis=1, keepdims=True)
        acc_scr[n] = alpha * acc_prev + jnp.dot(pexp.astype(BF16), v_ref[:, cols(n)], preferred_element_type=F32)
        m_scr[n] = m_cur

    @pl.when(kj == qi)
    def _():
        for n in range(KV_HEADS):
            res = acc_scr[n] / l_scr[n]
            for g in range(group):
                h = n * group + g
                o_ref[:, h * ATT_HEAD_DIM:(h + 1) * ATT_HEAD_DIM] = res[g * tq:(g + 1) * tq].astype(o_ref.dtype)


def _att_prompt(q, kb, vb, bias):
    t, d = q.shape
    tile = min(ATT_TILE, t)
    kv = kb.shape[1]
    nqb = t // tile
    pairs = [(i, j) for i in range(nqb) for j in range(i + 1)]
    qidx = jnp.asarray(np.array([p[0] for p in pairs], np.int32))
    kidx = jnp.asarray(np.array([p[1] for p in pairs], np.int32))
    group = d // ATT_HEAD_DIM // KV_HEADS
    grid_spec = pltpu.PrefetchScalarGridSpec(
        num_scalar_prefetch=2,
        grid=(len(pairs),),
        in_specs=[pl.BlockSpec((tile, d), lambda p, qi, kj: (qi[p], 0)),
                  pl.BlockSpec((tile, kv), lambda p, qi, kj: (kj[p], 0)),
                  pl.BlockSpec((tile, kv), lambda p, qi, kj: (kj[p], 0)),
                  pl.BlockSpec((tile, tile), lambda p, qi, kj: (qi[p], kj[p]))],
        out_specs=pl.BlockSpec((tile, d), lambda p, qi, kj: (qi[p], 0)),
        scratch_shapes=[pltpu.VMEM((KV_HEADS, group * tile, ATT_HEAD_DIM), F32)] * 3,
    )
    return pl.pallas_call(
        _att_prompt_kernel,
        grid_spec=grid_spec,
        out_shape=jax.ShapeDtypeStruct((t, d), BF16),
        compiler_params=_cparams(1),
        name="dsa_attn_prompt",
    )(qidx, kidx, q, kb, vb, bias)


def _dsa_sample_kernel(pt_ref, qall_ref, wcol_ref, qg_ref, knew_i_ref, knew_ref, vnew_ref, *rest, n_pages, k_top):
    ki_pages = rest[:n_pages]
    k_pages = rest[n_pages:2 * n_pages]
    v_pages = rest[2 * n_pages:3 * n_pages]
    o_ref, skey_scr, k_scr, v_scr = rest[3 * n_pages:]
    t_new = knew_ref.shape[1]
    rows = qall_ref.shape[1]
    group_rows = qg_ref.shape[2]
    lane = lax.broadcasted_iota(I32, (t_new, LANES), 1)
    rowq = lax.broadcasted_iota(I32, (t_new, LANES), 0)
    new_ok = lane <= rowq

    def pad_page(x):
        return jnp.concatenate([x, jnp.zeros((PAGE_SIZE - t_new, x.shape[1]), x.dtype)], axis=0)

    qa = qall_ref[0].astype(BF16)
    wc = wcol_ref[0] * (IDX_HEADS ** -0.5)

    def block_scores(kblk):
        dots = lax.dot_general(qa, kblk.astype(BF16), (_NT, ((), ())), preferred_element_type=F32)
        part = jnp.maximum(dots * (IDX_DIM ** -0.5), 0.0) * wc
        sc = part[0:t_new]
        for h in range(1, rows // t_new):
            sc = sc + part[h * t_new:(h + 1) * t_new]
        return sc

    head_rows = lambda n: pl.ds(n, PAGE_SIZE, stride=KV_HEADS)
    head_cols = lambda n: slice(n * ATT_HEAD_DIM, (n + 1) * ATT_HEAD_DIM)
    for p in range(n_pages):
        skey_scr[p] = _sort_key(block_scores(ki_pages[p][0]))
        for n in range(KV_HEADS):
            k_scr[n, p * PAGE_SIZE:(p + 1) * PAGE_SIZE, :] = k_pages[p][head_rows(n), :].astype(BF16)
            v_scr[n, p * PAGE_SIZE:(p + 1) * PAGE_SIZE, :] = v_pages[p][head_rows(n), :].astype(BF16)
    skey_scr[n_pages] = _sort_key(jnp.where(new_ok, block_scores(pad_page(knew_i_ref[0])), NEG_INF))
    knew = pad_page(knew_ref[0]).astype(BF16)
    vnew = pad_page(vnew_ref[0]).astype(BF16)
    for n in range(KV_HEADS):
        k_scr[n, n_pages * PAGE_SIZE:(n_pages + 1) * PAGE_SIZE, :] = knew[:, head_cols(n)]
        v_scr[n, n_pages * PAGE_SIZE:(n_pages + 1) * PAGE_SIZE, :] = vnew[:, head_cols(n)]

    tau, need = _topk_threshold(skey_scr, n_pages + 1, float(k_top), t_new)
    tri = _tri_incl()
    eq_before = jnp.zeros((t_new, 1), F32)
    bias = []
    for p in range(n_pages + 1):
        sel, eq_before = _selected(skey_scr[p], tau, need, eq_before, tri)
        ok = sel if p < n_pages else sel & new_ok
        bias.append(jnp.where(ok, 0.0, NEG_INF))
    bias = jnp.concatenate(bias, axis=1)
    bias = jnp.concatenate([bias] * (group_rows // t_new), axis=0)

    for n in range(KV_HEADS):
        s = lax.dot_general(qg_ref[0, n], k_scr[n], (_NT, ((), ())), preferred_element_type=F32)
        s = s * (ATT_HEAD_DIM ** -0.5) + bias
        pexp = jnp.exp(s - jnp.max(s, axis=1, keepdims=True))
        acc = jnp.dot(pexp.astype(BF16), v_scr[n], preferred_element_type=F32)
        o_ref[0, n] = acc / jnp.sum(pexp, axis=1, keepdims=True)


def _dsa_sample(page_table, qall, wcol, qg, knew_i, knew, vnew, cache_kidx, cache_k, cache_v, k_top):
    b, n_pages = page_table.shape
    t_new = knew.shape[1]
    kv = knew.shape[2]
    rows = qall.shape[1]
    group_rows = qg.shape[2]
    per_seq = lambda shape: pl.BlockSpec((1,) + shape, lambda bi, pt: (bi,) + (0,) * len(shape))
    idx_page = lambda p: pl.BlockSpec((1, PAGE_SIZE, IDX_DIM), lambda bi, pt: (pt[bi, p], 0, 0))
    kv_page = lambda p: pl.BlockSpec((PAGE_SIZE * KV_HEADS, ATT_HEAD_DIM), lambda bi, pt: (pt[bi, p], 0))
    grid_spec = pltpu.PrefetchScalarGridSpec(
        num_scalar_prefetch=1,
        grid=(b,),
        in_specs=[per_seq((rows, IDX_DIM)), per_seq((rows, 1)), per_seq((KV_HEADS, group_rows, ATT_HEAD_DIM)),
                  per_seq((t_new, IDX_DIM)), per_seq((t_new, kv)), per_seq((t_new, kv))]
                 + [idx_page(p) for p in range(n_pages)]
                 + [kv_page(p) for p in range(n_pages)] * 2,
        out_specs=per_seq((KV_HEADS, group_rows, ATT_HEAD_DIM)),
        scratch_shapes=[pltpu.VMEM((n_pages + 1, t_new, LANES), I32),
                        pltpu.VMEM((KV_HEADS, (n_pages + 1) * PAGE_SIZE, ATT_HEAD_DIM), BF16),
                        pltpu.VMEM((KV_HEADS, (n_pages + 1) * PAGE_SIZE, ATT_HEAD_DIM), BF16)],
    )
    return pl.pallas_call(
        functools.partial(_dsa_sample_kernel, n_pages=n_pages, k_top=k_top),
        grid_spec=grid_spec,
        out_shape=jax.ShapeDtypeStruct((b, KV_HEADS, group_rows, ATT_HEAD_DIM), F32),
        compiler_params=_cparams(1),
        name="dsa_sample",
    )(page_table, qall, wcol, qg, knew_i, knew, vnew,
      *([cache_kidx] * n_pages), *([cache_k] * n_pages), *([cache_v] * n_pages))


def _mix_kernel(xa_ref, xb_ref, wa_ref, wb_ref, ga_ref, gb_ref, o_ref):
    pa = jnp.dot(xa_ref[...], wa_ref[...], preferred_element_type=F32)
    pb = jnp.dot(xb_ref[...], wb_ref[...], preferred_element_type=F32)
    o_ref[...] = (_sigmoid(ga_ref[...]) * pa + _sigmoid(gb_ref[...]) * pb).astype(o_ref.dtype)


def _mix(o_rw, o_att, wa, wb, z, off, d, tm=512, tn=512):
    m = o_rw.shape[0]
    tm = min(tm, m)
    nb = d // tn
    return pl.pallas_call(
        _mix_kernel,
        grid=(m // tm, nb),
        in_specs=[pl.BlockSpec((tm, d), lambda i, j: (i, 0)),
                  pl.BlockSpec((tm, d), lambda i, j: (i, 0)),
                  pl.BlockSpec((d, tn), lambda i, j: (0, j)),
                  pl.BlockSpec((d, tn), lambda i, j: (0, j)),
                  pl.BlockSpec((tm, tn), lambda i, j: (i, off["ga"] // tn + j)),
                  pl.BlockSpec((tm, tn), lambda i, j: (i, off["gb"] // tn + j))],
        out_specs=pl.BlockSpec((tm, tn), lambda i, j: (i, j)),
        out_shape=jax.ShapeDtypeStruct((m, d), BF16),
        compiler_params=_cparams(2),
        name="branch_mix",
    )(o_rw, o_att, wa, wb, z, z)


def _conv_kernel(ug_ref, uv_ref, halo_ref, st0_ref, st1_ref, cw_ref, cb_ref, o_ref, *, seq_len):
    i = pl.program_id(0)
    x = ug_ref[...]
    acc = cb_ref[...] + x * cw_ref[CONV_W - 1:CONV_W, :]
    for back in range(1, CONV_W):
        prev = _prev_rows(x, halo_ref, [st0_ref, st1_ref], i, seq_len, back)
        acc = acc + prev * cw_ref[CONV_W - 1 - back:CONV_W - back, :]
    o_ref[...] = (acc * _sigmoid(acc) * uv_ref[...]).astype(o_ref.dtype)


def _conv_ffn(u, conv_prev, conv_w, conv_b, seq_len, d_ff, tm=256, tn=512):
    m = u.shape[0]
    tm = min(tm, m)
    nb = d_ff // tn
    assert conv_prev.shape[1] == CONV_W - 1 == 2
    if seq_len >= tm:
        st_spec = pl.BlockSpec((1, tn), lambda i, j: ((i * tm) // seq_len, j))
    else:
        st_spec = pl.BlockSpec((tm // seq_len, tn), lambda i, j: (i, j))
    return pl.pallas_call(
        functools.partial(_conv_kernel, seq_len=seq_len),
        grid=(m // tm, nb),
        in_specs=[pl.BlockSpec((tm, tn), lambda i, j: (i, j)),
                  pl.BlockSpec((tm, tn), lambda i, j: (i, nb + j)),
                  pl.BlockSpec((SUBLANES, tn), lambda i, j: (jnp.maximum(i * (tm // SUBLANES) - 1, 0), j)),
                  st_spec, st_spec,
                  pl.BlockSpec((CONV_W, tn), lambda i, j: (0, j)),
                  pl.BlockSpec((1, tn), lambda i, j: (0, j))],
        out_specs=pl.BlockSpec((tm, tn), lambda i, j: (i, j)),
        out_shape=jax.ShapeDtypeStruct((m, d_ff), BF16),
        compiler_params=_cparams(2),
        name="conv_ffn",
    )(u, u, u, conv_prev[:, 0], conv_prev[:, 1], conv_w, conv_b)


def _head_major(x, n_seq, seq_len, n_heads):
    return x.reshape(n_seq, seq_len, n_heads, RW_HEAD).transpose(0, 2, 1, 3).reshape(n_seq * n_heads, seq_len, RW_HEAD)


def _layer(x, mod, shift_prev, s0, conv_prev, W, cache):
    n_seq, seq_len, d = x.shape
    m = n_seq * seq_len
    off = W["off"]
    n_rw = d // RW_HEAD
    kv = KV_HEADS * ATT_HEAD_DIM
    xf = x.reshape(m, d)

    h = _norm_mod(xf, W["norm1_g"], mod, 1, 0, seq_len)
    z = _matmul(h, W["w_in"], tm=1024, tn=1024, name="in_proj")

    lora = lambda a: jnp.concatenate(
        [a[..., 3 * d:3 * d + DECAY_LORA], jnp.zeros(a.shape[:-1] + (LANES - DECAY_LORA,), a.dtype),
         a[..., 3 * d + DECAY_LORA:3 * d + DECAY_LORA + ICLR_LORA],
         jnp.zeros(a.shape[:-1] + (LANES - ICLR_LORA,), a.dtype),
         a[..., 3 * d + DECAY_LORA + ICLR_LORA:]], axis=-1)
    pack_rw = lambda a: jnp.concatenate([a[..., :3 * d], lora(a)], axis=-1)
    state_packed = pack_rw(shift_prev)
    r, k_raw, v_rw, decay, iclr, gate = _rwkv_prep(
        z, off, state_packed, W["mu_packed"], W["w0"], W["w_decay_up"], W["a0"], W["w_a_up"], W["w_g_up"],
        seq_len, d)
    if cache is None:
        assert n_seq == 1 and seq_len % SCAN_CHUNK == 0
        o_rw, s_pairs = _rwkv_scan_chunked((r, decay, k_raw, v_rw, iclr, gate), W["rw_rows"])
        s_new = jnp.stack([s_pairs[:, :RW_HEAD, :RW_HEAD], s_pairs[:, RW_HEAD:, RW_HEAD:]], axis=1).reshape(
            n_seq, n_rw, RW_HEAD, RW_HEAD)
    else:
        seqs = [_head_major(a, n_seq, seq_len, n_rw) for a in (r, decay, k_raw, v_rw, iclr, gate)]
        o_heads, s_new = _rwkv_scan(seqs, s0.reshape(n_seq * n_rw, RW_HEAD, RW_HEAD), W["rw_params"], n_rw, seq_len)
        o_rw = o_heads.reshape(n_seq, n_rw, seq_len, RW_HEAD).transpose(0, 2, 1, 3).reshape(m, d).astype(BF16)
        s_new = s_new.reshape(n_seq, n_rw, RW_HEAD, RW_HEAD)
    lora_cols = z[:, off["lora"]:off["lora"] + 512].reshape(n_seq, seq_len, 512)[:, -1]
    shift_new = jnp.concatenate(
        [z[:, off["r"]:off["r"] + 3 * d].reshape(n_seq, seq_len, 3 * d)[:, -1],
         lora_cols[:, :DECAY_LORA], lora_cols[:, LANES:LANES + ICLR_LORA], lora_cols[:, 2 * LANES:]], axis=-1)

    qn, k_new, k_bf = _dsa_prep(z, off, W["q_norm_g"], W["k_norm_g"], d)
    v_new = z[:, off["vv"]:off["vv"] + kv]
    ki_new = z[:, off["ki2"]:off["ki2"] + IDX_DIM]
    if cache is None:
        assert n_seq == 1
        k_top = min(TOPK_MAX, seq_len // 4)
        bias = _sel_prompt(z, off, k_top)
        o_att = _att_prompt(qn, k_bf, v_new.astype(BF16), bias)
    else:
        cache_k, cache_v, cache_kidx, page_table = cache
        n_pages = page_table.shape[1]
        k_top = min(TOPK_MAX, (n_pages * PAGE_SIZE + seq_len) // 4)
        n_pool = cache_k.shape[0]
        qi = z[:, off["qi"]:off["qi"] + IDX_HEADS * IDX_DIM]
        qall = qi.reshape(n_seq, seq_len, IDX_HEADS, IDX_DIM).transpose(0, 2, 1, 3).reshape(
            n_seq, IDX_HEADS * seq_len, IDX_DIM)
        wi = z[:, off["wi"]:off["wi"] + IDX_HEADS]
        wcol = wi.reshape(n_seq, seq_len, IDX_HEADS).transpose(0, 2, 1).reshape(n_seq, IDX_HEADS * seq_len, 1)
        group = d // ATT_HEAD_DIM // KV_HEADS
        qg = qn.reshape(n_seq, seq_len, KV_HEADS, group, ATT_HEAD_DIM).transpose(0, 2, 3, 1, 4).reshape(
            n_seq, KV_HEADS, group * seq_len, ATT_HEAD_DIM)
        o_g = _dsa_sample(page_table, qall, wcol, qg, ki_new.reshape(n_seq, seq_len, IDX_DIM),
                          k_new.reshape(n_seq, seq_len, kv), v_new.reshape(n_seq, seq_len, kv), cache_kidx,
                          cache_k.reshape(n_pool * PAGE_SIZE * KV_HEADS, ATT_HEAD_DIM),
                          cache_v.reshape(n_pool * PAGE_SIZE * KV_HEADS, ATT_HEAD_DIM), k_top)
        o_att = o_g.reshape(n_seq, KV_HEADS, group, seq_len, ATT_HEAD_DIM).transpose(0, 3, 1, 2, 4).reshape(
            m, d).astype(BF16)

    mixed = _mix(o_rw, o_att, W["w_proj_a"], W["w_proj_b"], z, off, d)
    x1 = _matmul(mixed, W["w_out"], tm=512, tn=512, res=xf, gate=mod, gate_col=2, seq_len=seq_len, name="out_proj")

    h2 = _norm_mod(x1, W["norm2_g"], mod, 4, 3, seq_len)
    u = _matmul(h2, W["w_ffn_up"], tm=1024, tn=1024, name="ffn_up")
    d_ff = W["w_ffn_down"].shape[0]
    act = _conv_ffn(u, conv_prev, W["conv_w"], W["conv_b"], seq_len, d_ff)
    x2 = _matmul(act, W["w_ffn_down"], tm=512, tn=512, res=x1, gate=mod, gate_col=5, seq_len=seq_len, name="ffn_down")
    ug = u[:, :d_ff].reshape(n_seq, seq_len, d_ff)
    conv_new = jnp.concatenate([conv_prev.astype(u.dtype), ug], axis=1)[:, -(CONV_W - 1):]

    hd = lambda a, w: a.reshape(n_seq, seq_len, KV_HEADS, w)
    return (x2.reshape(n_seq, seq_len, d), hd(k_new, ATT_HEAD_DIM), hd(v_new, ATT_HEAD_DIM),
            ki_new.reshape(n_seq, seq_len, IDX_DIM), s_new, shift_new, conv_new)


def kernel(x_prompt, x_sample, c_prompt, c_sample, cache_k, cache_v, cache_kidx, page_table, state_rwkv,
           state_shift, state_conv, w_ada, b_ada, norm1_g, norm2_g, w_in, shift_mu, w0, w_decay_up, a0, w_a_up,
           w_g_up, k_k, k_a, r_k, lnx_w, lnx_b, q_norm_g, k_norm_g, w_proj_a, w_proj_b, w_out, w_ffn_up, conv_w,
           conv_b, w_ffn_down):
    depth = w_in.shape[0]
    d = x_prompt.shape[-1]
    bp, bs = x_prompt.shape[0], x_sample.shape[0]
    off, _ = _in_layout(d)
    n_rw = d // RW_HEAD
    yp, ys = x_prompt, x_sample
    st_p, st_s = [], []
    pad_rows = lambda a, rows: jnp.concatenate([a, jnp.zeros((rows - a.shape[0],) + a.shape[1:], a.dtype)], axis=0)
    for l in range(depth):
        heads = lambda a: a.reshape(n_rw, 1, RW_HEAD)
        mu = shift_mu[l]
        mu_lora = jnp.concatenate(
            [mu[3 * d:3 * d + DECAY_LORA], jnp.zeros((LANES - DECAY_LORA,), F32),
             mu[3 * d + DECAY_LORA:3 * d + DECAY_LORA + ICLR_LORA], jnp.zeros((LANES - ICLR_LORA,), F32),
             mu[3 * d + DECAY_LORA + ICLR_LORA:]])
        W = {
            "off": off,
            "norm1_g": norm1_g[l][None], "norm2_g": norm2_g[l][None],
            "w_in": _pack_cols(w_in[l], d).astype(BF16),
            "mu_packed": jnp.concatenate([mu[:3 * d], mu_lora])[None],
            "w0": w0[l][None], "a0": a0[l][None],
            "w_decay_up": pad_rows(w_decay_up[l], LANES).astype(BF16),
            "w_a_up": pad_rows(w_a_up[l], LANES).astype(BF16),
            "w_g_up": w_g_up[l].astype(BF16),
            "rw_params": [heads(k_k[l]), heads(k_a[l]), heads(r_k[l]), heads(lnx_w[l]), heads(lnx_b[l])],
            "rw_rows": [a.reshape(1, d) for a in (k_k[l], k_a[l], r_k[l], lnx_w[l], lnx_b[l])],
            "q_norm_g": q_norm_g[l][None], "k_norm_g": k_norm_g[l][None],
            "w_proj_a": w_proj_a[l].astype(BF16), "w_proj_b": w_proj_b[l].astype(BF16),
            "w_out": w_out[l].astype(BF16), "w_ffn_up": w_ffn_up[l].astype(BF16),
            "conv_w": conv_w[l], "conv_b": conv_b[l][None], "w_ffn_down": w_ffn_down[l].astype(BF16),
        }
        c_all = jnp.concatenate([c_prompt, c_sample], axis=0)
        rows = -(-c_all.shape[0] // SUBLANES) * SUBLANES
        mod = _matmul(pad_rows(c_all, rows), w_ada[l].astype(BF16), tm=rows, tn=1024, bias=b_ada[l][None],
                      pre_silu=True, name="ada_mod")
        mod_p, mod_s = mod[:bp], mod[bp:bp + bs]
        d_ff = w_ffn_down.shape[1]
        yp, *sp = _layer(yp, mod_p, jnp.zeros((bp, state_shift.shape[-1]), F32),
                         jnp.zeros((bp, n_rw, RW_HEAD, RW_HEAD), F32), jnp.zeros((bp, CONV_W - 1, d_ff), F32),
                         W, None)
        ys, *ss = _layer(ys, mod_s, state_shift[l], state_rwkv[l], state_conv[l], W,
                         (cache_k[l], cache_v[l], cache_kidx[l], page_table))
        st_p.append(sp)
        st_s.append(ss)
    stk = lambda sts, i: jnp.stack([s[i] for s in sts], axis=0)
    return (yp, ys,
            stk(st_p, 0), stk(st_p, 1), stk(st_p, 2), stk(st_p, 3), stk(st_p, 4), stk(st_p, 5),
            stk(st_s, 0), stk(st_s, 1), stk(st_s, 2), stk(st_s, 3), stk(st_s, 4), stk(st_s, 5))
```

```python
import functools

import numpy as np
import jax
import jax.numpy as jnp
from jax import lax
from jax.experimental import pallas as pl
from jax.experimental.pallas import tpu as pltpu

F32 = jnp.float32
BF16 = jnp.bfloat16
I32 = jnp.int32

NORM_EPS = 1e-6
LNX_EPS = 64e-5
RW_HEAD = 64
ATT_HEAD_DIM = 128
KV_HEADS = 4
IDX_HEADS = 16
IDX_DIM = 64
PAGE_SIZE = 128
TOPK_MAX = 256
CONV_W = 3
DECAY_LORA = 96
ICLR_LORA = 96
GATE_LORA = 256

LANES = 128
SUBLANES = 8
VMEM_LIMIT = 56 * 1024 * 1024
NEG_INF = float("-inf")
INT_MIN = -2 ** 31


def _cparams(n_axes):
    return pltpu.CompilerParams(dimension_semantics=("arbitrary",) * n_axes,
                                vmem_limit_bytes=VMEM_LIMIT)


def _expand_rows(v, tm):
    s, w = v.shape
    if s == 1:
        return v
    return jnp.broadcast_to(v[:, None, :], (s, tm // s, w)).reshape(tm, w)


def _sigmoid(x):
    return 1.0 / (1.0 + jnp.exp(-x))


def _in_layout(d):
    rw = d
    att = d
    kv = KV_HEADS * ATT_HEAD_DIM
    qi = IDX_HEADS * IDX_DIM
    off = {}
    pos = 0
    for name, width in (("ga", d), ("gb", d), ("r", rw), ("k", rw), ("v", rw), ("q", att), ("qi", qi),
                        ("kk", kv), ("vv", kv), ("lora", 512), ("ki2", LANES), ("wi", LANES)):
        assert pos % width == 0, (name, pos, width)
        off[name] = pos
        pos += width
    total = -(-pos // 1024) * 1024
    return off, total


def _pack_cols(w, d):
    rw, att = d, d
    kv = KV_HEADS * ATT_HEAD_DIM
    qi = IDX_HEADS * IDX_DIM
    sizes = (rw, rw, rw, DECAY_LORA, ICLR_LORA, GATE_LORA, att, kv, kv, qi, IDX_DIM, IDX_HEADS, d, d)
    cuts = np.cumsum((0,) + sizes)
    seg = {n: w[..., cuts[i]:cuts[i + 1]] for i, n in enumerate(
        ("r", "k", "v", "wd", "ad", "gd", "q", "kk", "vv", "qi", "ki", "wi", "ga", "gb"))}
    z = lambda n: jnp.zeros(w.shape[:-1] + (n,), w.dtype)
    _, total = _in_layout(d)
    parts = [seg["ga"], seg["gb"], seg["r"], seg["k"], seg["v"], seg["q"], seg["qi"], seg["kk"], seg["vv"],
             seg["wd"], z(LANES - DECAY_LORA), seg["ad"], z(LANES - ICLR_LORA), seg["gd"],
             seg["ki"], seg["ki"], seg["wi"], z(LANES - IDX_HEADS)]
    used = sum(p.shape[-1] for p in parts)
    parts.append(z(total - used))
    return jnp.concatenate(parts, axis=-1)


def _mm_kernel(*refs, pre_silu, has_bias, has_res):
    x_ref, w_ref = refs[0], refs[1]
    o_ref = refs[-1]
    x = x_ref[...]
    if pre_silu:
        x = x.astype(F32)
        x = x * _sigmoid(x)
    acc = jnp.dot(x.astype(BF16), w_ref[...], preferred_element_type=F32)
    nxt = 2
    if has_bias:
        acc = acc + refs[nxt][...]
        nxt += 1
    if has_res:
        res_ref, gate_ref = refs[nxt], refs[nxt + 1]
        acc = res_ref[...] + _expand_rows(gate_ref[...], acc.shape[0]) * acc
    o_ref[...] = acc.astype(o_ref.dtype)


def _matmul(x, w, *, tm, tn, bias=None, pre_silu=False, res=None, gate=None, gate_col=0, seq_len=None,
            out_dtype=F32, name="mm"):
    m, k = x.shape
    n = w.shape[1]
    tm = min(tm, m)
    assert m % tm == 0 and n % tn == 0, (m, tm, n, tn)
    in_specs = [pl.BlockSpec((tm, k), lambda i, j: (i, 0)),
                pl.BlockSpec((k, tn), lambda i, j: (0, j))]
    args = [x, w]
    if bias is not None:
        in_specs.append(pl.BlockSpec((1, tn), lambda i, j: (0, j)))
        args.append(bias)
    if res is not None:
        in_specs.append(pl.BlockSpec((tm, tn), lambda i, j: (i, j)))
        args.append(res)
        gcol = gate_col * (n // tn)
        if seq_len >= tm:
            in_specs.append(pl.BlockSpec((1, tn), lambda i, j: ((i * tm) // seq_len, gcol + j)))
        else:
            in_specs.append(pl.BlockSpec((tm // seq_len, tn), lambda i, j: (i, gcol + j)))
        args.append(gate)
    return pl.pallas_call(
        functools.partial(_mm_kernel, pre_silu=pre_silu, has_bias=bias is not None, has_res=res is not None),
        grid=(m // tm, n // tn),
        in_specs=in_specs,
        out_specs=pl.BlockSpec((tm, tn), lambda i, j: (i, j)),
        out_shape=jax.ShapeDtypeStruct((m, n), out_dtype),
        compiler_params=_cparams(2),
        name=name,
    )(*args)


def _norm_mod_kernel(x_ref, g_ref, sc_ref, sh_ref, o_ref):
    x = x_ref[...]
    tm = x.shape[0]
    y = x * lax.rsqrt(jnp.mean(x * x, axis=-1, keepdims=True) + NORM_EPS)
    y = y * g_ref[...]
    y = y * (1.0 + _expand_rows(sc_ref[...], tm)) + _expand_rows(sh_ref[...], tm)
    o_ref[...] = y.astype(o_ref.dtype)


def _mod_spec(tm, width, col, seq_len):
    if seq_len >= tm:
        return pl.BlockSpec((1, width), lambda i: ((i * tm) // seq_len, col))
    return pl.BlockSpec((tm // seq_len, width), lambda i: (i, col))


def _norm_mod(x, g, mod, sc_col, sh_col, seq_len, tm=256):
    m, d = x.shape
    tm = min(tm, m)
    return pl.pallas_call(
        _norm_mod_kernel,
        grid=(m // tm,),
        in_specs=[pl.BlockSpec((tm, d), lambda i: (i, 0)),
                  pl.BlockSpec((1, d), lambda i: (0, 0)),
                  _mod_spec(tm, d, sc_col, seq_len),
                  _mod_spec(tm, d, sh_col, seq_len)],
        out_specs=pl.BlockSpec((tm, d), lambda i: (i, 0)),
        out_shape=jax.ShapeDtypeStruct((m, d), BF16),
        compiler_params=_cparams(1),
        name="norm_mod",
    )(x, g, mod, mod)


def _prev_rows(x, halo_ref, state_refs, i, seq_len, shift):
    tm = x.shape[0]
    row = lax.broadcasted_iota(I32, (tm, 1), 0)
    out = pltpu.roll(x, shift, 0)
    n_state = len(state_refs)
    for r in range(shift):
        back = shift - r
        state_row = state_refs[n_state - back][...]
        if seq_len >= tm:
            src = jnp.where(i == 0, state_row, halo_ref[SUBLANES - back:SUBLANES - back + 1, :])
            out = jnp.where(row == r, src, out)
        else:
            out = jnp.where(row % seq_len == r, _expand_rows(state_row, tm), out)
    return out


def _halo_spec(tm, width, col):
    return pl.BlockSpec((SUBLANES, width), lambda i: (jnp.maximum(i * (tm // SUBLANES) - 1, 0), col))


def _state_spec(tm, width, col, seq_len):
    if seq_len >= tm:
        return pl.BlockSpec((1, width), lambda i: ((i * tm) // seq_len, col))
    return pl.BlockSpec((tm // seq_len, width), lambda i: (i, col))


def _softplus(x):
    return jnp.maximum(x, 0.0) + jnp.log1p(jnp.exp(-jnp.abs(x)))


def _rwkv_prep_kernel(zr_ref, zk_ref, zv_ref, zl_ref, hr_ref, hk_ref, hv_ref, hl_ref,
                      sr_ref, sk_ref, sv_ref, sl_ref, mur_ref, muk_ref, muv_ref, mul_ref,
                      w0_ref, wdec_ref, a0_ref, wa_ref, wg_ref,
                      r_ref, k_ref, v_ref, dec_ref, ic_ref, g_ref, *, seq_len):
    i = pl.program_id(0)

    def shifted(z_ref, h_ref, s_ref, mu_ref):
        x = z_ref[...]
        prev = _prev_rows(x, h_ref, [s_ref], i, seq_len, 1)
        return x + (prev - x) * mu_ref[...]

    r_ref[...] = shifted(zr_ref, hr_ref, sr_ref, mur_ref)
    k_ref[...] = shifted(zk_ref, hk_ref, sk_ref, muk_ref)
    v_ref[...] = shifted(zv_ref, hv_ref, sv_ref, muv_ref)
    zl = shifted(zl_ref, hl_ref, sl_ref, mul_ref)
    wd = zl[:, 0:LANES]
    ad = zl[:, LANES:2 * LANES]
    gd = zl[:, 2 * LANES:4 * LANES]
    mm = lambda a, w_ref: jnp.dot(a.astype(BF16), w_ref[...], preferred_element_type=F32)
    dec_in = w0_ref[...] + mm(jnp.tanh(wd), wdec_ref)
    w_log = -_softplus(-dec_in) - 0.5
    dec_ref[...] = -jnp.exp(w_log)
    ic_ref[...] = _sigmoid(a0_ref[...] + mm(ad, wa_ref))
    g_ref[...] = mm(_sigmoid(gd), wg_ref)


def _rwkv_prep(z, off, state_packed, mu_packed, w0, wdec, a0, wa, wg, seq_len, d, tm=256):
    m = z.shape[0]
    tm = min(tm, m)
    cb = lambda name, width: off[name] // width
    zspec = lambda name, width: pl.BlockSpec((tm, width), lambda i: (i, cb(name, width)))
    sspec = lambda c, width: _state_spec(tm, width, c, seq_len)
    mspec = lambda c, width: pl.BlockSpec((1, width), lambda i: (0, c))
    full = lambda a: pl.BlockSpec(a.shape, lambda i: (0,) * a.ndim)
    lcol = 3 * d // 512
    out_spec = pl.BlockSpec((tm, d), lambda i: (i, 0))
    out_sds = jax.ShapeDtypeStruct((m, d), F32)
    return pl.pallas_call(
        functools.partial(_rwkv_prep_kernel, seq_len=seq_len),
        grid=(m // tm,),
        in_specs=[zspec("r", d), zspec("k", d), zspec("v", d), zspec("lora", 512),
                  _halo_spec(tm, d, cb("r", d)), _halo_spec(tm, d, cb("k", d)), _halo_spec(tm, d, cb("v", d)),
                  _halo_spec(tm, 512, cb("lora", 512)),
                  sspec(0, d), sspec(1, d), sspec(2, d), sspec(lcol, 512),
                  mspec(0, d), mspec(1, d), mspec(2, d), mspec(lcol, 512),
                  full(w0), full(wdec), full(a0), full(wa), full(wg)],
        out_specs=[out_spec] * 6,
        out_shape=[out_sds] * 6,
        compiler_params=_cparams(1),
        name="rwkv_prep",
    )(z, z, z, z, z, z, z, z, state_packed, state_packed, state_packed, state_packed,
      mu_packed, mu_packed, mu_packed, mu_packed, w0, wdec, a0, wa, wg)


SCAN_CHUNK = 64


def _split_bf16(x):
    hi = x.astype(BF16)
    return hi, (x - hi.astype(F32)).astype(BF16)


def _dot3(a, b, contract):
    dg = lambda x, y: lax.dot_general(x, y, (contract, ((), ())), preferred_element_type=F32)
    return dg(a[0], b[0]) + (dg(a[0], b[1]) + dg(a[1], b[0]))


_NN = ((1,), (0,))
_NT = ((1,), (1,))
_TN = ((0,), (0,))


def _scan_chunk_kernel(r_ref, logw_ref, k_ref, v_ref, ic_ref, g_ref, kk_ref, ka_ref, rk_ref, lw_ref, lb_ref,
                       o_ref, sout_ref, sb_scr, o_scr, km_scr):
    ti = pl.program_id(1)
    tb, width = r_ref.shape
    n_pairs = width // LANES
    n_chunks = tb // SCAN_CHUNK
    c = SCAN_CHUNK
    n = RW_HEAD

    @pl.when(ti == 0)
    def _():
        sb_scr[...] = jnp.zeros_like(sb_scr)

    lane = lax.broadcasted_iota(I32, (1, LANES), 1)
    head0 = lane < n
    rowc = lax.broadcasted_iota(I32, (c, 1), 0)
    bf = lambda x: x.astype(BF16)
    dg = lambda x, y, contract: lax.dot_general(x, y, (contract, ((), ())), preferred_element_type=F32)

    def hsum(x):
        s0 = jnp.sum(jnp.where(head0, x, 0.0), axis=1, keepdims=True)
        s1 = jnp.sum(jnp.where(head0, 0.0, x), axis=1, keepdims=True)
        return jnp.where(head0, s0, s1)

    def stack(x):
        return jnp.concatenate([jnp.where(head0, x, 0.0), jnp.where(head0, 0.0, x)], axis=0)

    rl = lax.broadcasted_iota(I32, (2 * c, 2 * c), 0) % c
    cl = lax.broadcasted_iota(I32, (2 * c, 2 * c), 1) % c
    strict = cl < rl
    incl = cl <= rl

    def prepare(ci, pi):
        sl = slice(ci * c, (ci + 1) * c)
        cols = slice(pi * LANES, (pi + 1) * LANES)
        r, logw, kraw, v, ic = (ref[sl, cols] for ref in (r_ref, logw_ref, k_ref, v_ref, ic_ref))
        kk = kraw * kk_ref[:, cols]
        kk = kk / jnp.maximum(jnp.sqrt(hsum(kk * kk)), 1e-12)
        kmod = kraw * (1.0 + (ic - 1.0) * ka_ref[:, cols])
        km_scr[sl, cols] = kmod
        cum = logw
        sh = 1
        while sh < c:
            cum = cum + jnp.where(rowc >= sh, pltpu.roll(cum, sh, 0), 0.0)
            sh *= 2
        p_inv = jnp.exp(-cum)
        a_t = -kk * jnp.exp(cum - logw)
        r_t = r * jnp.exp(cum)
        b_t = kk * ic * p_inv
        k_t = kmod * p_inv
        ar = _split_bf16(jnp.concatenate([stack(a_t), stack(r_t)], axis=0))
        bk = _split_bf16(jnp.concatenate([stack(b_t), stack(k_t)], axis=0))
        vst = stack(v)
        g4 = dg(ar[0], bk[0], _NT)
        a_ab = jnp.where(strict, g4[0:2 * c, 0:2 * c], 0.0)
        a_ak = jnp.where(strict, g4[0:2 * c, 2 * c:4 * c], 0.0)
        a_rb = jnp.where(incl, g4[2 * c:4 * c, 0:2 * c], 0.0)
        a_rk = jnp.where(incl, g4[2 * c:4 * c, 2 * c:4 * c], 0.0)
        vs = _split_bf16(vst)
        return dict(ar=ar, bk=bk, vs=vs, mpow=a_ab, tm1=a_ab, a_rb=bf(a_rb), akv=dg(bf(a_ak), vs[0], _NN),
                    rkv=dg(bf(a_rk), vs[0], _NN), p_end=jnp.exp(cum[c - 1:c, :]))

    pre = [[prepare(ci, pi) for pi in range(n_pairs)] for ci in range(n_chunks)]
    items = [p for row in pre for p in row]
    span = 2
    while span < c:
        for p in items:
            mb = bf(p["mpow"])
            p["mpow"] = dg(mb, mb, _NN)
        for p in items:
            p["tm1"] = p["tm1"] + p["mpow"] + dg(bf(p["tm1"]), bf(p["mpow"]), _NN)
        span *= 2
    for p in items:
        p["tm1"] = bf(p["tm1"])

    sbs = [sb_scr[pi] for pi in range(n_pairs)]
    for ci in range(n_chunks):
        for pi in range(n_pairs):
            p = pre[ci][pi]
            xr = _dot3(p["ar"], _split_bf16(sbs[pi]), _NT)
            y = xr[0:2 * c] + p["akv"]
            ust = y + dg(p["tm1"], bf(y), _NN)
            us = _split_bf16(ust)
            ost = xr[2 * c:4 * c] + dg(p["a_rb"], us[0], _NN) + p["rkv"]
            o_scr[ci * c:(ci + 1) * c, pi * LANES:(pi + 1) * LANES] = ost[0:c] + ost[c:2 * c]
            uv = (jnp.concatenate([us[0], p["vs"][0]], axis=0), jnp.concatenate([us[1], p["vs"][1]], axis=0))
            sbs[pi] = (sbs[pi] + _dot3(uv, p["bk"], _TN)) * p["p_end"]
    for pi in range(n_pairs):
        sb_scr[pi] = sbs[pi]
        sout_ref[pi] = sbs[pi]

    for pi in range(n_pairs):
        cols = slice(pi * LANES, (pi + 1) * LANES)
        o = o_scr[:, cols]
        mu = hsum(o) * (1.0 / n)
        var = hsum(jnp.square(o - mu)) * (1.0 / n)
        on = (o - mu) * lax.rsqrt(var + LNX_EPS) * lw_ref[:, cols] + lb_ref[:, cols]
        bonus = hsum(r_ref[:, cols] * km_scr[:, cols] * rk_ref[:, cols]) * v_ref[:, cols]
        o_ref[:, cols] = ((on + bonus) * g_ref[:, cols]).astype(o_ref.dtype)


SCAN_PAIRS = 2


def _rwkv_scan_chunked(seqs, params, tb=512):
    t_tot, d = seqs[0].shape
    tb = min(tb, t_tot)
    assert t_tot % tb == 0 and tb % SCAN_CHUNK == 0 and d % LANES == 0
    n_pairs = d // LANES
    width = SCAN_PAIRS * LANES
    assert d % width == 0
    seq_spec = pl.BlockSpec((tb, width), lambda hp, ti: (ti, hp))
    par_spec = pl.BlockSpec((1, width), lambda hp, ti: (0, hp))
    return pl.pallas_call(
        _scan_chunk_kernel,
        grid=(d // width, t_tot // tb),
        in_specs=[seq_spec] * 6 + [par_spec] * 5,
        out_specs=[seq_spec, pl.BlockSpec((SCAN_PAIRS, LANES, LANES), lambda hp, ti: (hp, 0, 0))],
        out_shape=[jax.ShapeDtypeStruct((t_tot, d), BF16), jax.ShapeDtypeStruct((n_pairs, LANES, LANES), F32)],
        scratch_shapes=[pltpu.VMEM((SCAN_PAIRS, LANES, LANES), F32), pltpu.VMEM((tb, width), F32),
                        pltpu.VMEM((tb, width), F32)],
        compiler_params=_cparams(2),
        name="rwkv_scan_chunked",
    )(*seqs, *params)


SCAN_GROUPS = 32


def _scan_groups_kernel(r_ref, logw_ref, k_ref, v_ref, ic_ref, g_ref, s0_ref, kk_ref, ka_ref, rk_ref, lw_ref, lb_ref,
                        o_ref, sout_ref):
    gb, t, n = r_ref.shape
    rows = gb * t
    bf = lambda x: x.astype(BF16)
    dg = lambda x, y, contract: lax.dot_general(x, y, (contract, ((), ())), preferred_element_type=F32)
    flat = lambda ref: ref[...].reshape(rows, n)
    per_row = lambda ref: jnp.broadcast_to(ref[...], (gb, t, n)).reshape(rows, n)
    r, logw, kraw, v, ic = flat(r_ref), flat(logw_ref), flat(k_ref), flat(v_ref), flat(ic_ref)
    kk = kraw * per_row(kk_ref)
    kk = kk / jnp.maximum(jnp.sqrt(jnp.sum(kk * kk, axis=1, keepdims=True)), 1e-12)
    kmod = kraw * (1.0 + (ic - 1.0) * per_row(ka_ref))
    pos = lax.broadcasted_iota(I32, (rows, 1), 0) % t
    cum = logw
    sh = 1
    while sh < t:
        cum = cum + jnp.where(pos >= sh, pltpu.roll(cum, sh, 0), 0.0)
        sh *= 2
    p_inv = jnp.exp(-cum)
    a_t = -kk * jnp.exp(cum - logw)
    r_t = r * jnp.exp(cum)
    b_t = kk * ic * p_inv
    k_t = kmod * p_inv
    p_end = jnp.exp(cum)

    ri = lax.broadcasted_iota(I32, (rows, rows), 0)
    ci = lax.broadcasted_iota(I32, (rows, rows), 1)
    same = (ri // t) == (ci // t)
    strict = same & (ci < ri)
    incl = same & (ci <= ri)
    g4 = dg(bf(jnp.concatenate([a_t, r_t], axis=0)), bf(jnp.concatenate([b_t, k_t], axis=0)), _NT)
    a_ab = jnp.where(strict, g4[0:rows, 0:rows], 0.0)
    a_ak = jnp.where(strict, g4[0:rows, rows:2 * rows], 0.0)
    a_rb = jnp.where(incl, g4[rows:2 * rows, 0:rows], 0.0)
    a_rk = jnp.where(incl, g4[rows:2 * rows, rows:2 * rows], 0.0)
    mpow = a_ab
    tm1 = a_ab
    span = 2
    while span < t:
        mb = bf(mpow)
        mpow = dg(mb, mb, _NN)
        tm1 = tm1 + mpow + dg(bf(tm1), bf(mpow), _NN)
        span *= 2
    vb = bf(v)
    akv = dg(bf(a_ak), vb, _NN)
    rkv = dg(bf(a_rk), vb, _NN)

    states = [s0_ref[gi] for gi in range(gb)]
    xs, rss = [], []
    for gi in range(gb):
        sl = slice(gi * t, (gi + 1) * t)
        ar = _split_bf16(jnp.concatenate([a_t[sl], r_t[sl]], axis=0))
        xr = _dot3(ar, _split_bf16(states[gi]), _NT)
        xs.append(xr[0:t])
        rss.append(xr[t:2 * t])
    y = jnp.concatenate(xs, axis=0) + akv
    ust = y + dg(bf(tm1), bf(y), _NN)
    o = jnp.concatenate(rss, axis=0) + dg(bf(a_rb), bf(ust), _NN) + rkv
    for gi in range(gb):
        sl = slice(gi * t, (gi + 1) * t)
        uv = _split_bf16(jnp.concatenate([ust[sl], v[sl]], axis=0))
        bk = _split_bf16(jnp.concatenate([b_t[sl], k_t[sl]], axis=0))
        sout_ref[gi] = (states[gi] + _dot3(uv, bk, _TN)) * p_end[(gi + 1) * t - 1:(gi + 1) * t, :]

    mu = jnp.mean(o, axis=1, keepdims=True)
    var = jnp.mean(jnp.square(o - mu), axis=1, keepdims=True)
    on = (o - mu) * lax.rsqrt(var + LNX_EPS) * per_row(lw_ref) + per_row(lb_ref)
    bonus = jnp.sum(r * kmod * per_row(rk_ref), axis=1, keepdims=True) * v
    o_ref[...] = ((on + bonus) * flat(g_ref)).reshape(gb, t, n)


def _rwkv_scan_groups(seqs, s0, params):
    g_tot, t, n = seqs[0].shape
    n_heads = params[0].shape[0]
    gb = SCAN_GROUPS
    assert g_tot % gb == 0 and n_heads % gb == 0 and t % SUBLANES == 0
    hblocks = n_heads // gb
    seq_spec = pl.BlockSpec((gb, t, n), lambda gi: (gi, 0, 0))
    st_spec = pl.BlockSpec((gb, n, n), lambda gi: (gi, 0, 0))
    par_spec = pl.BlockSpec((gb, 1, n), lambda gi: (gi % hblocks, 0, 0))
    return pl.pallas_call(
        _scan_groups_kernel,
        grid=(g_tot // gb,),
        in_specs=[seq_spec] * 6 + [st_spec] + [par_spec] * 5,
        out_specs=[seq_spec, st_spec],
        out_shape=[jax.ShapeDtypeStruct((g_tot, t, n), F32), jax.ShapeDtypeStruct((g_tot, n, n), F32)],
        compiler_params=_cparams(1),
        name="rwkv_scan_groups",
    )(*seqs, s0, *params)


def _dsa_prep_kernel(q_ref, k_ref, qg_ref, kg_ref, qo_ref, ko_ref, kb_ref):
    def head_norm(x, g):
        outs = []
        for h in range(x.shape[1] // ATT_HEAD_DIM):
            xh = x[:, h * ATT_HEAD_DIM:(h + 1) * ATT_HEAD_DIM]
            outs.append(xh * lax.rsqrt(jnp.mean(xh * xh, axis=-1, keepdims=True) + NORM_EPS) * g)
        return jnp.concatenate(outs, axis=1)

    qo_ref[...] = head_norm(q_ref[...], qg_ref[...]).astype(qo_ref.dtype)
    kn = head_norm(k_ref[...], kg_ref[...])
    ko_ref[...] = kn
    kb_ref[...] = kn.astype(kb_ref.dtype)


def _dsa_prep(z, off, qg, kg, d, tm=256):
    m = z.shape[0]
    tm = min(tm, m)
    kv = KV_HEADS * ATT_HEAD_DIM
    return pl.pallas_call(
        _dsa_prep_kernel,
        grid=(m // tm,),
        in_specs=[pl.BlockSpec((tm, d), lambda i: (i, off["q"] // d)),
                  pl.BlockSpec((tm, kv), lambda i: (i, off["kk"] // kv)),
                  pl.BlockSpec((1, ATT_HEAD_DIM), lambda i: (0, 0)),
                  pl.BlockSpec((1, ATT_HEAD_DIM), lambda i: (0, 0))],
        out_specs=[pl.BlockSpec((tm, d), lambda i: (i, 0)),
                   pl.BlockSpec((tm, kv), lambda i: (i, 0)),
                   pl.BlockSpec((tm, kv), lambda i: (i, 0))],
        out_shape=[jax.ShapeDtypeStruct((m, d), BF16), jax.ShapeDtypeStruct((m, kv), F32),
                   jax.ShapeDtypeStruct((m, kv), BF16)],
        compiler_params=_cparams(1),
        name="dsa_prep",
    )(z, z, qg, kg)


def _sort_key(x):
    x = jnp.where(x == 0.0, 0.0, x)
    bits = pltpu.bitcast(x, I32)
    return bits ^ (lax.shift_right_arithmetic(bits, 31) & 0x7FFFFFFF)


def _topk_threshold(skey_ref, n_blocks, k_top, rows):
    def count(pred_fn):
        hit = lambda j: jnp.where(pred_fn(skey_ref[j]), 1.0, 0.0)
        if isinstance(n_blocks, int):
            acc = hit(0)
            for j in range(1, n_blocks):
                acc = acc + hit(j)
        else:
            body = lambda j2, acc: acc + (hit(2 * j2) + hit(2 * j2 + 1))
            acc = lax.fori_loop(0, (n_blocks + 1) // 2, body, jnp.zeros((rows, LANES), F32))
        return jnp.sum(acc, axis=1, keepdims=True)

    def bit_body(it, tau):
        cand = tau + lax.shift_left(jnp.int32(1), 31 - it)
        return jnp.where(count(lambda s: s >= cand) >= k_top, cand, tau)

    def two_bit_body(it, tau):
        hi = lax.shift_left(jnp.int32(1), 31 - 2 * it)
        lo = lax.shift_left(jnp.int32(1), 30 - 2 * it)
        c1, c2, c3 = tau + lo, tau + hi, tau + hi + lo
        n1, n2, n3 = (count(lambda s, c=c: s >= c) for c in (c1, c2, c3))
        return jnp.where(n3 >= k_top, c3, jnp.where(n2 >= k_top, c2, jnp.where(n1 >= k_top, c1, tau)))

    tau0 = jnp.full((rows, 1), INT_MIN, I32)
    if isinstance(n_blocks, int):
        tau = lax.fori_loop(0, 16, two_bit_body, tau0)
    else:
        tau = lax.fori_loop(0, 32, bit_body, tau0)
    need = k_top - count(lambda s: s > tau)
    return tau, need, count(lambda s: s == tau)


def _selected(skey, tau, need, eq_before, tri):
    eq = skey == tau
    eqf = jnp.where(eq, 1.0, 0.0)
    prefix = eq_before + jnp.dot(eqf.astype(BF16), tri, preferred_element_type=F32)
    sel = (skey > tau) | (eq & (prefix <= need))
    return sel, eq_before + jnp.sum(eqf, axis=1, keepdims=True)


def _tri_incl():
    return (lax.broadcasted_iota(I32, (LANES, LANES), 0) <= lax.broadcasted_iota(I32, (LANES, LANES), 1)).astype(BF16)


def _sel_prompt_kernel(qi_ref, wi_ref, ki_ref, bias_ref, skey_scr, qm_scr, wb_scr, *, k_top):
    i = pl.program_id(0)
    qb = qi_ref.shape[0]
    nkb = bias_ref.shape[1] // LANES
    kcols = lambda j: pl.ds(pl.multiple_of(j * LANES, LANES), LANES)
    lane = lax.broadcasted_iota(I32, (qb, LANES), 1)
    rowq = lax.broadcasted_iota(I32, (qb, LANES), 0)
    wi = wi_ref[...] * (IDX_HEADS ** -0.5)
    for hp in range(IDX_HEADS // 2):
        qpair = qi_ref[:, hp * LANES:(hp + 1) * LANES] * (IDX_DIM ** -0.5)
        qm_scr[2 * hp * qb:(2 * hp + 1) * qb, :] = jnp.where(lane < IDX_DIM, qpair, 0.0).astype(BF16)
        qm_scr[(2 * hp + 1) * qb:(2 * hp + 2) * qb, :] = jnp.where(lane >= IDX_DIM, qpair, 0.0).astype(BF16)
    for h in range(IDX_HEADS):
        wb_scr[h] = jnp.broadcast_to(wi[:, h:h + 1], (qb, LANES))

    def score_blocks(j2, carry):
        kblk = ki_ref[pl.ds(pl.multiple_of(j2 * 2 * LANES, 2 * LANES), 2 * LANES), :].astype(BF16)
        dots = lax.dot_general(qm_scr[...], kblk, (_NT, ((), ())), preferred_element_type=F32)
        for e in range(2):
            sc = jnp.zeros((qb, LANES), F32)
            for h in range(IDX_HEADS):
                sc = sc + jnp.maximum(dots[h * qb:(h + 1) * qb, e * LANES:(e + 1) * LANES], 0.0) * wb_scr[h]
            j = 2 * j2 + e
            allowed = (j < i) | ((j == i) & (lane <= rowq))
            skey_scr[j] = _sort_key(jnp.where(allowed, sc, NEG_INF))
        return carry

    lax.fori_loop(0, (i + 2) // 2, score_blocks, 0)
    tau, need, n_eq = _topk_threshold(skey_scr, i + 1, float(k_top), qb)

    def write_with_ties():
        tri = _tri_incl()

        def write_block(j, eq_before):
            sel, eq_after = _selected(skey_scr[j], tau, need, eq_before, tri)
            allowed = (j < i) | (lane <= rowq)
            bias_ref[:, kcols(j)] = jnp.where(sel & allowed, 0.0, NEG_INF).astype(bias_ref.dtype)
            return eq_after

        lax.fori_loop(0, i + 1, write_block, jnp.zeros((qb, 1), F32))

    def write_no_ties():
        def write_block(j, carry):
            allowed = (j < i) | (lane <= rowq)
            bias_ref[:, kcols(j)] = jnp.where((skey_scr[j] >= tau) & allowed, 0.0, NEG_INF).astype(bias_ref.dtype)
            return carry

        lax.fori_loop(0, i + 1, write_block, 0)

    surplus_ties = jnp.max(jnp.where(n_eq > need, 1.0, 0.0)) > 0.0
    lax.cond(surplus_ties, write_with_ties, write_no_ties)

    def fill_block(j, carry):
        bias_ref[:, kcols(j)] = jnp.full((qb, LANES), NEG_INF, bias_ref.dtype)
        return carry

    lax.fori_loop(i + 1, nkb, fill_block, 0)


def _sel_prompt(z, off, k_top):
    t = z.shape[0]
    qb = LANES
    nkb = t // LANES
    nqb = t // qb
    qi_w = IDX_HEADS * IDX_DIM
    return pl.pallas_call(
        functools.partial(_sel_prompt_kernel, k_top=k_top),
        grid=(nqb,),
        in_specs=[pl.BlockSpec((qb, qi_w), lambda i: (i, off["qi"] // qi_w)),
                  pl.BlockSpec((qb, LANES), lambda i: (i, off["wi"] // LANES)),
                  pl.BlockSpec((t, LANES), lambda i: (0, off["ki2"] // LANES))],
        out_specs=pl.BlockSpec((qb, t), lambda i: (i, 0)),
        out_shape=jax.ShapeDtypeStruct((t, t), BF16),
        scratch_shapes=[pltpu.VMEM((nkb, qb, LANES), I32),
                        pltpu.VMEM((IDX_HEADS * qb, LANES), BF16),
                        pltpu.VMEM((IDX_HEADS, qb, LANES), F32)],
        compiler_params=_cparams(1),
        name="dsa_select_prompt",
    )(z, z, z)


ATT_TILE = 256


def _att_prompt_kernel(qidx_ref, kidx_ref, q_ref, k_ref, v_ref, bias_ref, o_ref, m_scr, l_scr, acc_scr):
    p = pl.program_id(0)
    qi = qidx_ref[p]
    kj = kidx_ref[p]
    tq = q_ref.shape[0]
    n_heads = q_ref.shape[1] // ATT_HEAD_DIM
    group = n_heads // KV_HEADS
    first = kj == 0
    bias = bias_ref[...].astype(F32)
    bias = jnp.concatenate([bias] * group, axis=0)
    for n in range(KV_HEADS):
        cols = lambda h: slice(h * ATT_HEAD_DIM, (h + 1) * ATT_HEAD_DIM)
        q = jnp.concatenate([q_ref[:, cols(n * group + g)] for g in range(group)], axis=0)
        s = lax.dot_general(q, k_ref[:, cols(n)], (_NT, ((), ())), preferred_element_type=F32)
        s = s * (ATT_HEAD_DIM ** -0.5) + bias
        m_prev = jnp.where(first, NEG_INF, m_scr[n])
        l_prev = jnp.where(first, 0.0, l_scr[n])
        acc_prev = jnp.where(first, 0.0, acc_scr[n])
        m_cur = jnp.maximum(m_prev, jnp.max(s, axis=1, keepdims=True))
        m_safe = jnp.where(m_cur == NEG_INF, 0.0, m_cur)
        pexp = jnp.exp(s - jnp.concatenate([m_safe] * (s.shape[1] // LANES), axis=1))
        alpha = jnp.exp(m_prev - m_safe)
        l_scr[n] = alpha * l_prev + jnp.sum(pexp, axis=1, keepdims=True)
        acc_scr[n] = alpha * acc_prev + jnp.dot(pexp.astype(BF16), v_ref[:, cols(n)], preferred_element_type=F32)
        m_scr[n] = m_cur

    @pl.when(kj == qi)
    def _():
        for n in range(KV_HEADS):
            res = acc_scr[n] / l_scr[n]
            for g in range(group):
                h = n * group + g
                o_ref[:, h * ATT_HEAD_DIM:(h + 1) * ATT_HEAD_DIM] = res[g * tq:(g + 1) * tq].astype(o_ref.dtype)


def _att_prompt(q, kb, vb, bias):
    t, d = q.shape
    tile = min(ATT_TILE, t)
    kv = kb.shape[1]
    nqb = t // tile
    pairs = [(i, j) for i in range(nqb) for j in range(i + 1)]
    qidx = jnp.asarray(np.array([p[0] for p in pairs], np.int32))
    kidx = jnp.asarray(np.array([p[1] for p in pairs], np.int32))
    group = d // ATT_HEAD_DIM // KV_HEADS
    grid_spec = pltpu.PrefetchScalarGridSpec(
        num_scalar_prefetch=2,
        grid=(len(pairs),),
        in_specs=[pl.BlockSpec((tile, d), lambda p, qi, kj: (qi[p], 0)),
                  pl.BlockSpec((tile, kv), lambda p, qi, kj: (kj[p], 0)),
                  pl.BlockSpec((tile, kv), lambda p, qi, kj: (kj[p], 0)),
                  pl.BlockSpec((tile, tile), lambda p, qi, kj: (qi[p], kj[p]))],
        out_specs=pl.BlockSpec((tile, d), lambda p, qi, kj: (qi[p], 0)),
        scratch_shapes=[pltpu.VMEM((KV_HEADS, group * tile, ATT_HEAD_DIM), F32)] * 3,
    )
    return pl.pallas_call(
        _att_prompt_kernel,
        grid_spec=grid_spec,
        out_shape=jax.ShapeDtypeStruct((t, d), BF16),
        compiler_params=_cparams(1),
        name="dsa_attn_prompt",
    )(qidx, kidx, q, kb, vb, bias)


def _dsa_sample_kernel(pt_ref, qall_ref, wcol_ref, qg_ref, knew_i_ref, knew_ref, vnew_ref, *rest, n_pages, k_top):
    ki_pages = rest[:n_pages]
    k_pages = rest[n_pages:2 * n_pages]
    v_pages = rest[2 * n_pages:3 * n_pages]
    o_ref, skey_scr, k_scr, v_scr = rest[3 * n_pages:]
    t_new = knew_ref.shape[1]
    rows = qall_ref.shape[1]
    group_rows = qg_ref.shape[2]
    lane = lax.broadcasted_iota(I32, (t_new, LANES), 1)
    rowq = lax.broadcasted_iota(I32, (t_new, LANES), 0)
    new_ok = lane <= rowq

    def pad_page(x):
        return jnp.concatenate([x, jnp.zeros((PAGE_SIZE - t_new, x.shape[1]), x.dtype)], axis=0)

    qa = qall_ref[0].astype(BF16)
    wc = wcol_ref[0] * (IDX_HEADS ** -0.5)

    def block_scores(kblk):
        dots = lax.dot_general(qa, kblk.astype(BF16), (_NT, ((), ())), preferred_element_type=F32)
        part = jnp.maximum(dots * (IDX_DIM ** -0.5), 0.0) * wc
        sc = part[0:t_new]
        for h in range(1, rows // t_new):
            sc = sc + part[h * t_new:(h + 1) * t_new]
        return sc

    head_rows = lambda n: pl.ds(n, PAGE_SIZE, stride=KV_HEADS)
    head_cols = lambda n: slice(n * ATT_HEAD_DIM, (n + 1) * ATT_HEAD_DIM)
    for p in range(n_pages):
        skey_scr[p] = _sort_key(block_scores(ki_pages[p][0]))
        for n in range(KV_HEADS):
            k_scr[n, p * PAGE_SIZE:(p + 1) * PAGE_SIZE, :] = k_pages[p][head_rows(n), :].astype(BF16)
            v_scr[n, p * PAGE_SIZE:(p + 1) * PAGE_SIZE, :] = v_pages[p][head_rows(n), :].astype(BF16)
    skey_scr[n_pages] = _sort_key(jnp.where(new_ok, block_scores(pad_page(knew_i_ref[0])), NEG_INF))
    knew = pad_page(knew_ref[0]).astype(BF16)
    vnew = pad_page(vnew_ref[0]).astype(BF16)
    for n in range(KV_HEADS):
        k_scr[n, n_pages * PAGE_SIZE:(n_pages + 1) * PAGE_SIZE, :] = knew[:, head_cols(n)]
        v_scr[n, n_pages * PAGE_SIZE:(n_pages + 1) * PAGE_SIZE, :] = vnew[:, head_cols(n)]

    tau, need, _ = _topk_threshold(skey_scr, n_pages + 1, float(k_top), t_new)
    tri = _tri_incl()
    eq_before = jnp.zeros((t_new, 1), F32)
    bias = []
    for p in range(n_pages + 1):
        sel, eq_before = _selected(skey_scr[p], tau, need, eq_before, tri)
        ok = sel if p < n_pages else sel & new_ok
        bias.append(jnp.where(ok, 0.0, NEG_INF))
    bias = jnp.concatenate(bias, axis=1)
    bias = jnp.concatenate([bias] * (group_rows // t_new), axis=0)

    for n in range(KV_HEADS):
        s = lax.dot_general(qg_ref[0, n], k_scr[n], (_NT, ((), ())), preferred_element_type=F32)
        s = s * (ATT_HEAD_DIM ** -0.5) + bias
        pexp = jnp.exp(s - jnp.max(s, axis=1, keepdims=True))
        acc = jnp.dot(pexp.astype(BF16), v_scr[n], preferred_element_type=F32)
        o_ref[0, n] = acc / jnp.sum(pexp, axis=1, keepdims=True)


def _dsa_sample(page_table, qall, wcol, qg, knew_i, knew, vnew, cache_kidx, cache_k, cache_v, k_top):
    b, n_pages = page_table.shape
    t_new = knew.shape[1]
    kv = knew.shape[2]
    rows = qall.shape[1]
    group_rows = qg.shape[2]
    per_seq = lambda shape: pl.BlockSpec((1,) + shape, lambda bi, pt: (bi,) + (0,) * len(shape))
    idx_page = lambda p: pl.BlockSpec((1, PAGE_SIZE, IDX_DIM), lambda bi, pt: (pt[bi, p], 0, 0))
    kv_page = lambda p: pl.BlockSpec((PAGE_SIZE * KV_HEADS, ATT_HEAD_DIM), lambda bi, pt: (pt[bi, p], 0))
    grid_spec = pltpu.PrefetchScalarGridSpec(
        num_scalar_prefetch=1,
        grid=(b,),
        in_specs=[per_seq((rows, IDX_DIM)), per_seq((rows, 1)), per_seq((KV_HEADS, group_rows, ATT_HEAD_DIM)),
                  per_seq((t_new, IDX_DIM)), per_seq((t_new, kv)), per_seq((t_new, kv))]
                 + [idx_page(p) for p in range(n_pages)]
                 + [kv_page(p) for p in range(n_pages)] * 2,
        out_specs=per_seq((KV_HEADS, group_rows, ATT_HEAD_DIM)),
        scratch_shapes=[pltpu.VMEM((n_pages + 1, t_new, LANES), I32),
                        pltpu.VMEM((KV_HEADS, (n_pages + 1) * PAGE_SIZE, ATT_HEAD_DIM), BF16),
                        pltpu.VMEM((KV_HEADS, (n_pages + 1) * PAGE_SIZE, ATT_HEAD_DIM), BF16)],
    )
    return pl.pallas_call(
        functools.partial(_dsa_sample_kernel, n_pages=n_pages, k_top=k_top),
        grid_spec=grid_spec,
        out_shape=jax.ShapeDtypeStruct((b, KV_HEADS, group_rows, ATT_HEAD_DIM), F32),
        compiler_params=_cparams(1),
        name="dsa_sample",
    )(page_table, qall, wcol, qg, knew_i, knew, vnew,
      *([cache_kidx] * n_pages), *([cache_k] * n_pages), *([cache_v] * n_pages))


def _mix_kernel(xa_ref, xb_ref, wa_ref, wb_ref, ga_ref, gb_ref, o_ref):
    pa = jnp.dot(xa_ref[...], wa_ref[...], preferred_element_type=F32)
    pb = jnp.dot(xb_ref[...], wb_ref[...], preferred_element_type=F32)
    o_ref[...] = (_sigmoid(ga_ref[...]) * pa + _sigmoid(gb_ref[...]) * pb).astype(o_ref.dtype)


def _mix(o_rw, o_att, wa, wb, z, off, d, tm=512, tn=512):
    m = o_rw.shape[0]
    tm = min(tm, m)
    nb = d // tn
    return pl.pallas_call(
        _mix_kernel,
        grid=(m // tm, nb),
        in_specs=[pl.BlockSpec((tm, d), lambda i, j: (i, 0)),
                  pl.BlockSpec((tm, d), lambda i, j: (i, 0)),
                  pl.BlockSpec((d, tn), lambda i, j: (0, j)),
                  pl.BlockSpec((d, tn), lambda i, j: (0, j)),
                  pl.BlockSpec((tm, tn), lambda i, j: (i, off["ga"] // tn + j)),
                  pl.BlockSpec((tm, tn), lambda i, j: (i, off["gb"] // tn + j))],
        out_specs=pl.BlockSpec((tm, tn), lambda i, j: (i, j)),
        out_shape=jax.ShapeDtypeStruct((m, d), BF16),
        compiler_params=_cparams(2),
        name="branch_mix",
    )(o_rw, o_att, wa, wb, z, z)


def _conv_kernel(ug_ref, uv_ref, halo_ref, st0_ref, st1_ref, cw_ref, cb_ref, o_ref, *, seq_len):
    i = pl.program_id(0)
    x = ug_ref[...]
    acc = cb_ref[...] + x * cw_ref[CONV_W - 1:CONV_W, :]
    for back in range(1, CONV_W):
        prev = _prev_rows(x, halo_ref, [st0_ref, st1_ref], i, seq_len, back)
        acc = acc + prev * cw_ref[CONV_W - 1 - back:CONV_W - back, :]
    o_ref[...] = (acc * _sigmoid(acc) * uv_ref[...]).astype(o_ref.dtype)


def _conv_ffn(u, conv_prev, conv_w, conv_b, seq_len, d_ff, tm=512, tn=1408):
    m = u.shape[0]
    tm = min(tm, m)
    assert d_ff % tn == 0 and tn % LANES == 0
    nb = d_ff // tn
    assert conv_prev.shape[1] == CONV_W - 1 == 2
    if seq_len >= tm:
        st_spec = pl.BlockSpec((1, tn), lambda i, j: ((i * tm) // seq_len, j))
    else:
        st_spec = pl.BlockSpec((tm // seq_len, tn), lambda i, j: (i, j))
    return pl.pallas_call(
        functools.partial(_conv_kernel, seq_len=seq_len),
        grid=(m // tm, nb),
        in_specs=[pl.BlockSpec((tm, tn), lambda i, j: (i, j)),
                  pl.BlockSpec((tm, tn), lambda i, j: (i, nb + j)),
                  pl.BlockSpec((SUBLANES, tn), lambda i, j: (jnp.maximum(i * (tm // SUBLANES) - 1, 0), j)),
                  st_spec, st_spec,
                  pl.BlockSpec((CONV_W, tn), lambda i, j: (0, j)),
                  pl.BlockSpec((1, tn), lambda i, j: (0, j))],
        out_specs=pl.BlockSpec((tm, tn), lambda i, j: (i, j)),
        out_shape=jax.ShapeDtypeStruct((m, d_ff), BF16),
        compiler_params=_cparams(2),
        name="conv_ffn",
    )(u, u, u, conv_prev[:, 0], conv_prev[:, 1], conv_w, conv_b)


def _head_major(x, n_seq, seq_len, n_heads):
    return x.reshape(n_seq, seq_len, n_heads, RW_HEAD).transpose(0, 2, 1, 3).reshape(n_seq * n_heads, seq_len, RW_HEAD)


def _layer(x, mod, shift_prev, s0, conv_prev, W, cache):
    n_seq, seq_len, d = x.shape
    m = n_seq * seq_len
    off = W["off"]
    n_rw = d // RW_HEAD
    kv = KV_HEADS * ATT_HEAD_DIM
    xf = x.reshape(m, d)

    h = _norm_mod(xf, W["norm1_g"], mod, 1, 0, seq_len)
    z = _matmul(h, W["w_in"], tm=1024, tn=1024, name="in_proj")

    lora = lambda a: jnp.concatenate(
        [a[..., 3 * d:3 * d + DECAY_LORA], jnp.zeros(a.shape[:-1] + (LANES - DECAY_LORA,), a.dtype),
         a[..., 3 * d + DECAY_LORA:3 * d + DECAY_LORA + ICLR_LORA],
         jnp.zeros(a.shape[:-1] + (LANES - ICLR_LORA,), a.dtype),
         a[..., 3 * d + DECAY_LORA + ICLR_LORA:]], axis=-1)
    pack_rw = lambda a: jnp.concatenate([a[..., :3 * d], lora(a)], axis=-1)
    state_packed = pack_rw(shift_prev)
    r, k_raw, v_rw, decay, iclr, gate = _rwkv_prep(
        z, off, state_packed, W["mu_packed"], W["w0"], W["w_decay_up"], W["a0"], W["w_a_up"], W["w_g_up"],
        seq_len, d)
    if cache is None:
        assert n_seq == 1 and seq_len % SCAN_CHUNK == 0
        o_rw, s_pairs = _rwkv_scan_chunked((r, decay, k_raw, v_rw, iclr, gate), W["rw_rows"])
        s_new = jnp.stack([s_pairs[:, :RW_HEAD, :RW_HEAD], s_pairs[:, RW_HEAD:, RW_HEAD:]], axis=1).reshape(
            n_seq, n_rw, RW_HEAD, RW_HEAD)
    else:
        seqs = [_head_major(a, n_seq, seq_len, n_rw) for a in (r, decay, k_raw, v_rw, iclr, gate)]
        o_heads, s_new = _rwkv_scan_groups(seqs, s0.reshape(n_seq * n_rw, RW_HEAD, RW_HEAD), W["rw_params"])
        o_rw = o_heads.reshape(n_seq, n_rw, seq_len, RW_HEAD).transpose(0, 2, 1, 3).reshape(m, d).astype(BF16)
        s_new = s_new.reshape(n_seq, n_rw, RW_HEAD, RW_HEAD)
    z_last = z.reshape(n_seq, seq_len, z.shape[1])[:, -1]
    lora_cols = z_last[:, off["lora"]:off["lora"] + 512]
    shift_new = jnp.concatenate(
        [z_last[:, off["r"]:off["r"] + 3 * d],
         lora_cols[:, :DECAY_LORA], lora_cols[:, LANES:LANES + ICLR_LORA], lora_cols[:, 2 * LANES:]], axis=-1)

    qn, k_new, k_bf = _dsa_prep(z, off, W["q_norm_g"], W["k_norm_g"], d)
    v_new = z[:, off["vv"]:off["vv"] + kv]
    ki_new = z[:, off["ki2"]:off["ki2"] + IDX_DIM]
    if cache is None:
        assert n_seq == 1
        k_top = min(TOPK_MAX, seq_len // 4)
        bias = _sel_prompt(z, off, k_top)
        o_att = _att_prompt(qn, k_bf, v_new.astype(BF16), bias)
    else:
        cache_k, cache_v, cache_kidx, page_table = cache
        n_pages = page_table.shape[1]
        k_top = min(TOPK_MAX, (n_pages * PAGE_SIZE + seq_len) // 4)
        n_pool = cache_k.shape[0]
        qi = z[:, off["qi"]:off["qi"] + IDX_HEADS * IDX_DIM]
        qall = qi.reshape(n_seq, seq_len, IDX_HEADS, IDX_DIM).transpose(0, 2, 1, 3).reshape(
            n_seq, IDX_HEADS * seq_len, IDX_DIM)
        wi = z[:, off["wi"]:off["wi"] + IDX_HEADS]
        wcol = wi.reshape(n_seq, seq_len, IDX_HEADS).transpose(0, 2, 1).reshape(n_seq, IDX_HEADS * seq_len, 1)
        group = d // ATT_HEAD_DIM // KV_HEADS
        qg = qn.reshape(n_seq, seq_len, KV_HEADS, group, ATT_HEAD_DIM).transpose(0, 2, 3, 1, 4).reshape(
            n_seq, KV_HEADS, group * seq_len, ATT_HEAD_DIM)
        o_g = _dsa_sample(page_table, qall, wcol, qg, ki_new.reshape(n_seq, seq_len, IDX_DIM),
                          k_new.reshape(n_seq, seq_len, kv), v_new.reshape(n_seq, seq_len, kv), cache_kidx,
                          cache_k.reshape(n_pool * PAGE_SIZE * KV_HEADS, ATT_HEAD_DIM),
                          cache_v.reshape(n_pool * PAGE_SIZE * KV_HEADS, ATT_HEAD_DIM), k_top)
        o_att = o_g.reshape(n_seq, KV_HEADS, group, seq_len, ATT_HEAD_DIM).transpose(0, 3, 1, 2, 4).reshape(
            m, d).astype(BF16)

    mixed = _mix(o_rw, o_att, W["w_proj_a"], W["w_proj_b"], z, off, d)
    x1 = _matmul(mixed, W["w_out"], tm=512, tn=512, res=xf, gate=mod, gate_col=2, seq_len=seq_len, name="out_proj")

    h2 = _norm_mod(x1, W["norm2_g"], mod, 4, 3, seq_len)
    u = _matmul(h2, W["w_ffn_up"], tm=1024, tn=1024, name="ffn_up")
    d_ff = W["w_ffn_down"].shape[0]
    act = _conv_ffn(u, conv_prev, W["conv_w"], W["conv_b"], seq_len, d_ff)
    x2 = _matmul(act, W["w_ffn_down"], tm=512, tn=512, res=x1, gate=mod, gate_col=5, seq_len=seq_len, name="ffn_down")
    assert seq_len >= CONV_W - 1
    conv_new = u.reshape(n_seq, seq_len, u.shape[1])[:, seq_len - (CONV_W - 1):, :d_ff]

    hd = lambda a, w: a.reshape(n_seq, seq_len, KV_HEADS, w)
    return (x2.reshape(n_seq, seq_len, d), hd(k_new, ATT_HEAD_DIM), hd(v_new, ATT_HEAD_DIM),
            ki_new.reshape(n_seq, seq_len, IDX_DIM), s_new, shift_new, conv_new)


def kernel(x_prompt, x_sample, c_prompt, c_sample, cache_k, cache_v, cache_kidx, page_table, state_rwkv,
           state_shift, state_conv, w_ada, b_ada, norm1_g, norm2_g, w_in, shift_mu, w0, w_decay_up, a0, w_a_up,
           w_g_up, k_k, k_a, r_k, lnx_w, lnx_b, q_norm_g, k_norm_g, w_proj_a, w_proj_b, w_out, w_ffn_up, conv_w,
           conv_b, w_ffn_down):
    depth = w_in.shape[0]
    d = x_prompt.shape[-1]
    bp, bs = x_prompt.shape[0], x_sample.shape[0]
    off, _ = _in_layout(d)
    n_rw = d // RW_HEAD
    yp, ys = x_prompt, x_sample
    st_p, st_s = [], []
    pad_rows = lambda a, rows: jnp.concatenate([a, jnp.zeros((rows - a.shape[0],) + a.shape[1:], a.dtype)], axis=0)
    for l in range(depth):
        heads = lambda a: a.reshape(n_rw, 1, RW_HEAD)
        mu = shift_mu[l]
        mu_lora = jnp.concatenate(
            [mu[3 * d:3 * d + DECAY_LORA], jnp.zeros((LANES - DECAY_LORA,), F32),
             mu[3 * d + DECAY_LORA:3 * d + DECAY_LORA + ICLR_LORA], jnp.zeros((LANES - ICLR_LORA,), F32),
             mu[3 * d + DECAY_LORA + ICLR_LORA:]])
        W = {
            "off": off,
            "norm1_g": norm1_g[l][None], "norm2_g": norm2_g[l][None],
            "w_in": _pack_cols(w_in[l], d).astype(BF16),
            "mu_packed": jnp.concatenate([mu[:3 * d], mu_lora])[None],
            "w0": w0[l][None], "a0": a0[l][None],
            "w_decay_up": pad_rows(w_decay_up[l], LANES).astype(BF16),
            "w_a_up": pad_rows(w_a_up[l], LANES).astype(BF16),
            "w_g_up": w_g_up[l].astype(BF16),
            "rw_params": [heads(k_k[l]), heads(k_a[l]), heads(r_k[l]), heads(lnx_w[l]), heads(lnx_b[l])],
            "rw_rows": [a.reshape(1, d) for a in (k_k[l], k_a[l], r_k[l], lnx_w[l], lnx_b[l])],
            "q_norm_g": q_norm_g[l][None], "k_norm_g": k_norm_g[l][None],
            "w_proj_a": w_proj_a[l].astype(BF16), "w_proj_b": w_proj_b[l].astype(BF16),
            "w_out": w_out[l].astype(BF16), "w_ffn_up": w_ffn_up[l].astype(BF16),
            "conv_w": conv_w[l], "conv_b": conv_b[l][None], "w_ffn_down": w_ffn_down[l].astype(BF16),
        }
        c_all = jnp.concatenate([c_prompt, c_sample], axis=0)
        rows = -(-c_all.shape[0] // SUBLANES) * SUBLANES
        mod = _matmul(pad_rows(c_all, rows), w_ada[l].astype(BF16), tm=rows, tn=1024, bias=b_ada[l][None],
                      pre_silu=True, name="ada_mod")
        mod_p, mod_s = mod[:bp], mod[bp:bp + bs]
        d_ff = w_ffn_down.shape[1]
        yp, *sp = _layer(yp, mod_p, jnp.zeros((bp, state_shift.shape[-1]), F32),
                         jnp.zeros((bp, n_rw, RW_HEAD, RW_HEAD), F32), jnp.zeros((bp, CONV_W - 1, d_ff), F32),
                         W, None)
        ys, *ss = _layer(ys, mod_s, state_shift[l], state_rwkv[l], state_conv[l], W,
                         (cache_k[l], cache_v[l], cache_kidx[l], page_table))
        st_p.append(sp)
        st_s.append(ss)
    stk = lambda sts, i: sts[0][i][None] if len(sts) == 1 else jnp.stack([s[i] for s in sts], axis=0)
    return (yp, ys,
            stk(st_p, 0), stk(st_p, 1), stk(st_p, 2), stk(st_p, 3), stk(st_p, 4), stk(st_p, 5),
            stk(st_s, 0), stk(st_s, 1), stk(st_s, 2), stk(st_s, 3), stk(st_s, 4), stk(st_s, 5))
```

```python
import functools

import numpy as np
import jax
import jax.numpy as jnp
from jax import lax
from jax.experimental import pallas as pl
from jax.experimental.pallas import tpu as pltpu

F32 = jnp.float32
BF16 = jnp.bfloat16
I32 = jnp.int32

NORM_EPS = 1e-6
LNX_EPS = 64e-5
RW_HEAD = 64
ATT_HEAD_DIM = 128
KV_HEADS = 4
IDX_HEADS = 16
IDX_DIM = 64
PAGE_SIZE = 128
TOPK_MAX = 256
CONV_W = 3
DECAY_LORA = 96
ICLR_LORA = 96
GATE_LORA = 256

LANES = 128
SUBLANES = 8
VMEM_LIMIT = 56 * 1024 * 1024
NEG_INF = float("-inf")
INT_MIN = -2 ** 31
QK_SCALE_LOG2 =ATT_HEAD_DIM ** -0.5 * 1.4426950408889634


def _cparams(n_axes):
    return pltpu.CompilerParams(dimension_semantics=("arbitrary",) * n_axes,
                                vmem_limit_bytes=VMEM_LIMIT)


def _expand_rows(v, tm):
    s, w = v.shape
    if s == 1:
        return v
    return jnp.broadcast_to(v[:, None, :], (s, tm // s, w)).reshape(tm, w)


def _sigmoid(x):
    return 1.0 / (1.0 + jnp.exp(-x))


def _in_layout(d):
    rw = d
    att = d
    kv = KV_HEADS * ATT_HEAD_DIM
    qi = IDX_HEADS * IDX_DIM
    off = {}
    pos = 0
    for name, width in (("ga", d), ("gb", d), ("r", rw), ("k", rw), ("v", rw), ("q", att), ("qi", qi),
                        ("kk", kv), ("vv", kv), ("lora", 512), ("ki2", LANES), ("wi", LANES)):
        assert pos % width == 0, (name, pos, width)
        off[name] = pos
        pos += width
    total = -(-pos // 1024) * 1024
    return off, total


def _pack_cols(w, d):
    rw, att = d, d
    kv = KV_HEADS * ATT_HEAD_DIM
    qi = IDX_HEADS * IDX_DIM
    sizes = (rw, rw, rw, DECAY_LORA, ICLR_LORA, GATE_LORA, att, kv, kv, qi, IDX_DIM, IDX_HEADS, d, d)
    cuts = np.cumsum((0,) + sizes)
    seg = {n: w[..., cuts[i]:cuts[i + 1]] for i, n in enumerate(
        ("r", "k", "v", "wd", "ad", "gd", "q", "kk", "vv", "qi", "ki", "wi", "ga", "gb"))}
    z = lambda n: jnp.zeros(w.shape[:-1] + (n,), w.dtype)
    _, total = _in_layout(d)
    parts = [seg["ga"], seg["gb"], seg["r"], seg["k"], seg["v"], seg["q"], seg["qi"], seg["kk"], seg["vv"],
             seg["wd"], z(LANES - DECAY_LORA), seg["ad"], z(LANES - ICLR_LORA), seg["gd"],
             seg["ki"], seg["ki"], seg["wi"], z(LANES - IDX_HEADS)]
    used = sum(p.shape[-1] for p in parts)
    parts.append(z(total - used))
    return jnp.concatenate(parts, axis=-1)


def _mm_kernel(*refs, pre_silu, has_bias, has_res):
    x_ref, w_ref = refs[0], refs[1]
    o_ref, wb_scr = refs[-2], refs[-1]

    @pl.when(pl.program_id(1) == 0)
    def _():
        wb_scr[...] = w_ref[...].astype(BF16)

    x = x_ref[...]
    if pre_silu:
        x = x.astype(F32)
        x = x * _sigmoid(x)
    acc = jnp.dot(x.astype(BF16), wb_scr[...], preferred_element_type=F32)
    nxt = 2
    if has_bias:
        acc = acc + refs[nxt][...]
        nxt += 1
    if has_res:
        res_ref, gate_ref = refs[nxt], refs[nxt + 1]
        acc = res_ref[...] + _expand_rows(gate_ref[...], acc.shape[0]) * acc
    o_ref[...] = acc.astype(o_ref.dtype)


def _matmul(x, w, *, tm, tn, bias=None, pre_silu=False, res=None, gate=None, gate_col=0, seq_len=None,
            out_dtype=F32, name="mm"):
    m, k = x.shape
    n = w.shape[1]
    tm = min(tm, m)
    assert m % tm == 0 and n % tn == 0, (m, tm, n, tn)
    in_specs = [pl.BlockSpec((tm, k), lambda j, i: (i, 0)),
                pl.BlockSpec((k, tn), lambda j, i: (0, j))]
    args = [x, w]
    if bias is not None:
        in_specs.append(pl.BlockSpec((1, tn), lambda j, i: (0, j)))
        args.append(bias)
    if res is not None:
        in_specs.append(pl.BlockSpec((tm, tn), lambda j, i: (i, j)))
        args.append(res)
        gcol = gate_col * (n // tn)
        if seq_len >= tm:
            in_specs.append(pl.BlockSpec((1, tn), lambda j, i: ((i * tm) // seq_len, gcol + j)))
        else:
            in_specs.append(pl.BlockSpec((tm // seq_len, tn), lambda j, i: (i, gcol + j)))
        args.append(gate)
    return pl.pallas_call(
        functools.partial(_mm_kernel, pre_silu=pre_silu, has_bias=bias is not None, has_res=res is not None),
        grid=(n // tn, m // tm),
        in_specs=in_specs,
        out_specs=pl.BlockSpec((tm, tn), lambda j, i: (i, j)),
        out_shape=jax.ShapeDtypeStruct((m, n), out_dtype),
        scratch_shapes=[pltpu.VMEM((k, tn), BF16)],
        compiler_params=_cparams(2),
        name=name,
    )(*args)


def _norm_mod_kernel(x_ref, g_ref, sc_ref, sh_ref, o_ref):
    x = x_ref[...]
    tm = x.shape[0]
    y = x * lax.rsqrt(jnp.mean(x * x, axis=-1, keepdims=True) + NORM_EPS)
    y = y * g_ref[...]
    y = y * (1.0 + _expand_rows(sc_ref[...], tm)) + _expand_rows(sh_ref[...], tm)
    o_ref[...] = y.astype(o_ref.dtype)


def _mod_spec(tm, width, col, seq_len):
    if seq_len >= tm:
        return pl.BlockSpec((1, width), lambda i: ((i * tm) // seq_len, col))
    return pl.BlockSpec((tm // seq_len, width), lambda i: (i, col))


def _norm_mod(x, g, mod, sc_col, sh_col, seq_len, tm=256):
    m, d = x.shape
    tm = min(tm, m)
    return pl.pallas_call(
        _norm_mod_kernel,
        grid=(m // tm,),
        in_specs=[pl.BlockSpec((tm, d), lambda i: (i, 0)),
                  pl.BlockSpec((1, d), lambda i: (0, 0)),
                  _mod_spec(tm, d, sc_col, seq_len),
                  _mod_spec(tm, d, sh_col, seq_len)],
        out_specs=pl.BlockSpec((tm, d), lambda i: (i, 0)),
        out_shape=jax.ShapeDtypeStruct((m, d), BF16),
        compiler_params=_cparams(1),
        name="norm_mod",
    )(x, g, mod, mod)


def _prev_rows(x, halo_ref, state_refs, i, seq_len, shift):
    tm = x.shape[0]
    row = lax.broadcasted_iota(I32, (tm, 1), 0)
    out = pltpu.roll(x, shift, 0)
    n_state = len(state_refs)
    for r in range(shift):
        back = shift - r
        state_row = state_refs[n_state - back][...]
        if seq_len >= tm:
            src = jnp.where(i == 0, state_row, halo_ref[SUBLANES - back:SUBLANES - back + 1, :])
            out = jnp.where(row == r, src, out)
        else:
            out = jnp.where(row % seq_len == r, _expand_rows(state_row, tm), out)
    return out


def _halo_spec(tm, width, col):
    return pl.BlockSpec((SUBLANES, width), lambda i: (jnp.maximum(i * (tm // SUBLANES) - 1, 0), col))


def _state_spec(tm, width, col, seq_len):
    if seq_len >= tm:
        return pl.BlockSpec((1, width), lambda i: ((i * tm) // seq_len, col))
    return pl.BlockSpec((tm // seq_len, width), lambda i: (i, col))


def _softplus(x):
    return jnp.maximum(x, 0.0) + jnp.log1p(jnp.exp(-jnp.abs(x)))


def _rwkv_prep_kernel(zr_ref, zk_ref, zv_ref, zl_ref, hr_ref, hk_ref, hv_ref, hl_ref,
                      sr_ref, sk_ref, sv_ref, sl_ref, mur_ref, muk_ref, muv_ref, mul_ref,
                      w0_ref, wdec_ref, a0_ref, wa_ref, wg_ref,
                      r_ref, k_ref, v_ref, dec_ref, ic_ref, g_ref, *, seq_len):
    i = pl.program_id(0)

    def shifted(z_ref, h_ref, s_ref, mu_ref):
        x = z_ref[...]
        prev = _prev_rows(x, h_ref, [s_ref], i, seq_len, 1)
        return x + (prev - x) * mu_ref[...]

    r_ref[...] = shifted(zr_ref, hr_ref, sr_ref, mur_ref)
    k_ref[...] = shifted(zk_ref, hk_ref, sk_ref, muk_ref)
    v_ref[...] = shifted(zv_ref, hv_ref, sv_ref, muv_ref)
    zl = shifted(zl_ref, hl_ref, sl_ref, mul_ref)
    wd = zl[:, 0:LANES]
    ad = zl[:, LANES:2 * LANES]
    gd = zl[:, 2 * LANES:4 * LANES]
    mm = lambda a, w_ref: jnp.dot(a.astype(BF16), w_ref[...], preferred_element_type=F32)
    dec_in = w0_ref[...] + mm(jnp.tanh(wd), wdec_ref)
    w_log = -_softplus(-dec_in) - 0.5
    dec_ref[...] = -jnp.exp(w_log)
    ic_ref[...] = _sigmoid(a0_ref[...] + mm(ad, wa_ref))
    g_ref[...] = mm(_sigmoid(gd), wg_ref)


def _rwkv_prep(z, off, state_packed, mu_packed, w0, wdec, a0, wa, wg, seq_len, d, tm=256):
    m = z.shape[0]
    tm = min(tm, m)
    cb = lambda name, width: off[name] // width
    zspec = lambda name, width: pl.BlockSpec((tm, width), lambda i: (i, cb(name, width)))
    sspec = lambda c, width: _state_spec(tm, width, c, seq_len)
    mspec = lambda c, width: pl.BlockSpec((1, width), lambda i: (0, c))
    full = lambda a: pl.BlockSpec(a.shape, lambda i: (0,) * a.ndim)
    lcol = 3 * d // 512
    out_spec = pl.BlockSpec((tm, d), lambda i: (i, 0))
    out_sds = jax.ShapeDtypeStruct((m, d), F32)
    return pl.pallas_call(
        functools.partial(_rwkv_prep_kernel, seq_len=seq_len),
        grid=(m // tm,),
        in_specs=[zspec("r", d), zspec("k", d), zspec("v", d), zspec("lora", 512),
                  _halo_spec(tm, d, cb("r", d)), _halo_spec(tm, d, cb("k", d)), _halo_spec(tm, d, cb("v", d)),
                  _halo_spec(tm, 512, cb("lora", 512)),
                  sspec(0, d), sspec(1, d), sspec(2, d), sspec(lcol, 512),
                  mspec(0, d), mspec(1, d), mspec(2, d), mspec(lcol, 512),
                  full(w0), full(wdec), full(a0), full(wa), full(wg)],
        out_specs=[out_spec] * 6,
        out_shape=[out_sds] * 6,
        compiler_params=_cparams(1),
        name="rwkv_prep",
    )(z, z, z, z, z, z, z, z, state_packed, state_packed, state_packed, state_packed,
      mu_packed, mu_packed, mu_packed, mu_packed, w0, wdec, a0, wa, wg)


SCAN_CHUNK = 64


def _split_bf16(x):
    hi = x.astype(BF16)
    return hi, (x - hi.astype(F32)).astype(BF16)


def _dot3(a, b, contract):
    dg = lambda x, y: lax.dot_general(x, y, (contract, ((), ())), preferred_element_type=F32)
    return dg(a[0], b[0]) + (dg(a[0], b[1]) + dg(a[1], b[0]))


_NN = ((1,), (0,))
_NT = ((1,), (1,))
_TN = ((0,), (0,))


def _scan_chunk_kernel(r_ref, logw_ref, k_ref, v_ref, ic_ref, g_ref, kk_ref, ka_ref, rk_ref, lw_ref, lb_ref,
                       o_ref, sout_ref, sb_scr, o_scr, km_scr):
    ti = pl.program_id(1)
    tb, width = r_ref.shape
    n_pairs = width // LANES
    n_chunks = tb // SCAN_CHUNK
    c = SCAN_CHUNK
    n = RW_HEAD

    @pl.when(ti == 0)
    def _():
        sb_scr[...] = jnp.zeros_like(sb_scr)

    lane = lax.broadcasted_iota(I32, (1, LANES), 1)
    head0 = lane < n
    rowc = lax.broadcasted_iota(I32, (c, 1), 0)
    bf = lambda x: x.astype(BF16)
    dg = lambda x, y, contract: lax.dot_general(x, y, (contract, ((), ())), preferred_element_type=F32)

    def hsum(x):
        s0 = jnp.sum(jnp.where(head0, x, 0.0), axis=1, keepdims=True)
        s1 = jnp.sum(jnp.where(head0, 0.0, x), axis=1, keepdims=True)
        return jnp.where(head0, s0, s1)

    def stack(x):
        return jnp.concatenate([jnp.where(head0, x, 0.0), jnp.where(head0, 0.0, x)], axis=0)

    rl = lax.broadcasted_iota(I32, (2 * c, 2 * c), 0) % c
    cl = lax.broadcasted_iota(I32, (2 * c, 2 * c), 1) % c
    strict = cl < rl
    incl = cl <= rl

    def prepare(ci, pi):
        sl = slice(ci * c, (ci + 1) * c)
        cols = slice(pi * LANES, (pi + 1) * LANES)
        r, logw, kraw, v, ic = (ref[sl, cols] for ref in (r_ref, logw_ref, k_ref, v_ref, ic_ref))
        kk = kraw * kk_ref[:, cols]
        kk = kk / jnp.maximum(jnp.sqrt(hsum(kk * kk)), 1e-12)
        kmod = kraw * (1.0 + (ic - 1.0) * ka_ref[:, cols])
        km_scr[sl, cols] = kmod
        cum = logw
        sh = 1
        while sh < c:
            cum = cum + jnp.where(rowc >= sh, pltpu.roll(cum, sh, 0), 0.0)
            sh *= 2
        p_inv = jnp.exp(-cum)
        a_t = -kk * jnp.exp(cum - logw)
        r_t = r * jnp.exp(cum)
        b_t = kk * ic * p_inv
        k_t = kmod * p_inv
        ar = _split_bf16(jnp.concatenate([stack(a_t), stack(r_t)], axis=0))
        bk = _split_bf16(jnp.concatenate([stack(b_t), stack(k_t)], axis=0))
        vst = stack(v)
        g4 = dg(ar[0], bk[0], _NT)
        a_ab = jnp.where(strict, g4[0:2 * c, 0:2 * c], 0.0)
        a_ak = jnp.where(strict, g4[0:2 * c, 2 * c:4 * c], 0.0)
        a_rb = jnp.where(incl, g4[2 * c:4 * c, 0:2 * c], 0.0)
        a_rk = jnp.where(incl, g4[2 * c:4 * c, 2 * c:4 * c], 0.0)
        vs = _split_bf16(vst)
        return dict(ar=ar, bk=bk, vs=vs, mpow=a_ab, tm1=a_ab, a_rb=bf(a_rb), akv=dg(bf(a_ak), vs[0], _NN),
                    rkv=dg(bf(a_rk), vs[0], _NN), p_end=jnp.exp(cum[c - 1:c, :]))

    pre = [[prepare(ci, pi) for pi in range(n_pairs)] for ci in range(n_chunks)]
    items = [p for row in pre for p in row]
    span = 2
    while span < c:
        for p in items:
            mb = bf(p["mpow"])
            p["mpow"] = dg(mb, mb, _NN)
        for p in items:
            p["tm1"] = p["tm1"] + p["mpow"] + dg(bf(p["tm1"]), bf(p["mpow"]), _NN)
        span *= 2
    for p in items:
        p["tm1"] = bf(p["tm1"])

    sbs = [sb_scr[pi] for pi in range(n_pairs)]
    for ci in range(n_chunks):
        for pi in range(n_pairs):
            p = pre[ci][pi]
            xr = _dot3(p["ar"], _split_bf16(sbs[pi]), _NT)
            y = xr[0:2 * c] + p["akv"]
            ust = y + dg(p["tm1"], bf(y), _NN)
            us = _split_bf16(ust)
            ost = xr[2 * c:4 * c] + dg(p["a_rb"], us[0], _NN) + p["rkv"]
            o_scr[ci * c:(ci + 1) * c, pi * LANES:(pi + 1) * LANES] = ost[0:c] + ost[c:2 * c]
            uv = (jnp.concatenate([us[0], p["vs"][0]], axis=0), jnp.concatenate([us[1], p["vs"][1]], axis=0))
            sbs[pi] = (sbs[pi] + _dot3(uv, p["bk"], _TN)) * p["p_end"]
    for pi in range(n_pairs):
        sb_scr[pi] = sbs[pi]
        sout_ref[pi] = sbs[pi]

    for pi in range(n_pairs):
        cols = slice(pi * LANES, (pi + 1) * LANES)
        o = o_scr[:, cols]
        mu = hsum(o) * (1.0 / n)
        var = hsum(jnp.square(o - mu)) * (1.0 / n)
        on = (o - mu) * lax.rsqrt(var + LNX_EPS) * lw_ref[:, cols] + lb_ref[:, cols]
        bonus = hsum(r_ref[:, cols] * km_scr[:, cols] * rk_ref[:, cols]) * v_ref[:, cols]
        o_ref[:, cols] = ((on + bonus) * g_ref[:, cols]).astype(o_ref.dtype)


SCAN_PAIRS = 2


def _rwkv_scan_chunked(seqs, params, tb=512):
    t_tot, d = seqs[0].shape
    tb = min(tb, t_tot)
    assert t_tot % tb == 0 and tb % SCAN_CHUNK == 0 and d % LANES == 0
    n_pairs = d // LANES
    width = SCAN_PAIRS * LANES
    assert d % width == 0
    seq_spec = pl.BlockSpec((tb, width), lambda hp, ti: (ti, hp))
    par_spec = pl.BlockSpec((1, width), lambda hp, ti: (0, hp))
    return pl.pallas_call(
        _scan_chunk_kernel,
        grid=(d // width, t_tot // tb),
        in_specs=[seq_spec] * 6 + [par_spec] * 5,
        out_specs=[seq_spec, pl.BlockSpec((SCAN_PAIRS, LANES, LANES), lambda hp, ti: (hp, 0, 0))],
        out_shape=[jax.ShapeDtypeStruct((t_tot, d), BF16), jax.ShapeDtypeStruct((n_pairs, LANES, LANES), F32)],
        scratch_shapes=[pltpu.VMEM((SCAN_PAIRS, LANES, LANES), F32), pltpu.VMEM((tb, width), F32),
                        pltpu.VMEM((tb, width), F32)],
        compiler_params=_cparams(2),
        name="rwkv_scan_chunked",
    )(*seqs, *params)


SCAN_GROUPS = 32


def _scan_groups_kernel(r_ref, logw_ref, k_ref, v_ref, ic_ref, g_ref, s0_ref, kk_ref, ka_ref, rk_ref, lw_ref, lb_ref,
                        o_ref, sout_ref):
    gb, t, n = r_ref.shape
    rows = gb * t
    bf = lambda x: x.astype(BF16)
    dg = lambda x, y, contract: lax.dot_general(x, y, (contract, ((), ())), preferred_element_type=F32)
    flat = lambda ref: ref[...].reshape(rows, n)
    per_row = lambda ref: jnp.broadcast_to(ref[...], (gb, t, n)).reshape(rows, n)
    r, logw, kraw, v, ic = flat(r_ref), flat(logw_ref), flat(k_ref), flat(v_ref), flat(ic_ref)
    kk = kraw * per_row(kk_ref)
    kk = kk / jnp.maximum(jnp.sqrt(jnp.sum(kk * kk, axis=1, keepdims=True)), 1e-12)
    kmod = kraw * (1.0 + (ic - 1.0) * per_row(ka_ref))
    pos = lax.broadcasted_iota(I32, (rows, 1), 0) % t
    cum = logw
    sh = 1
    while sh < t:
        cum = cum + jnp.where(pos >= sh, pltpu.roll(cum, sh, 0), 0.0)
        sh *= 2
    p_inv = jnp.exp(-cum)
    a_t = -kk * jnp.exp(cum - logw)
    r_t = r * jnp.exp(cum)
    b_t = kk * ic * p_inv
    k_t = kmod * p_inv
    p_end = jnp.exp(cum)

    ri = lax.broadcasted_iota(I32, (rows, rows), 0)
    ci = lax.broadcasted_iota(I32, (rows, rows), 1)
    same = (ri // t) == (ci // t)
    strict = same & (ci < ri)
    incl = same & (ci <= ri)
    g4 = dg(bf(jnp.concatenate([a_t, r_t], axis=0)), bf(jnp.concatenate([b_t, k_t], axis=0)), _NT)
    a_ab = jnp.where(strict, g4[0:rows, 0:rows], 0.0)
    a_ak = jnp.where(strict, g4[0:rows, rows:2 * rows], 0.0)
    a_rb = jnp.where(incl, g4[rows:2 * rows, 0:rows], 0.0)
    a_rk = jnp.where(incl, g4[rows:2 * rows, rows:2 * rows], 0.0)
    mpow = a_ab
    tm1 = a_ab
    span = 2
    while span < t:
        mb = bf(mpow)
        mpow = dg(mb, mb, _NN)
        tm1 = tm1 + mpow + dg(bf(tm1), bf(mpow), _NN)
        span *= 2
    vb = bf(v)
    akv = dg(bf(a_ak), vb, _NN)
    rkv = dg(bf(a_rk), vb, _NN)

    states = [s0_ref[gi] for gi in range(gb)]
    xs, rss = [], []
    for gi in range(gb):
        sl = slice(gi * t, (gi + 1) * t)
        ar = _split_bf16(jnp.concatenate([a_t[sl], r_t[sl]], axis=0))
        xr = _dot3(ar, _split_bf16(states[gi]), _NT)
        xs.append(xr[0:t])
        rss.append(xr[t:2 * t])
    y = jnp.concatenate(xs, axis=0) + akv
    ust = y + dg(bf(tm1), bf(y), _NN)
    o = jnp.concatenate(rss, axis=0) + dg(bf(a_rb), bf(ust), _NN) + rkv
    for gi in range(gb):
        sl = slice(gi * t, (gi + 1) * t)
        uv = _split_bf16(jnp.concatenate([ust[sl], v[sl]], axis=0))
        bk = _split_bf16(jnp.concatenate([b_t[sl], k_t[sl]], axis=0))
        sout_ref[gi] = (states[gi] + _dot3(uv, bk, _TN)) * p_end[(gi + 1) * t - 1:(gi + 1) * t, :]

    mu = jnp.mean(o, axis=1, keepdims=True)
    var = jnp.mean(jnp.square(o - mu), axis=1, keepdims=True)
    on = (o - mu) * lax.rsqrt(var + LNX_EPS) * per_row(lw_ref) + per_row(lb_ref)
    bonus = jnp.sum(r * kmod * per_row(rk_ref), axis=1, keepdims=True) * v
    o_ref[...] = ((on + bonus) * flat(g_ref)).reshape(gb, t, n)


def _rwkv_scan_groups(seqs, s0, params):
    g_tot, t, n = seqs[0].shape
    n_heads = params[0].shape[0]
    gb = SCAN_GROUPS
    assert g_tot % gb == 0 and n_heads % gb == 0 and t % SUBLANES == 0
    hblocks = n_heads // gb
    seq_spec = pl.BlockSpec((gb, t, n), lambda gi: (gi, 0, 0))
    st_spec = pl.BlockSpec((gb, n, n), lambda gi: (gi, 0, 0))
    par_spec = pl.BlockSpec((gb, 1, n), lambda gi: (gi % hblocks, 0, 0))
    return pl.pallas_call(
        _scan_groups_kernel,
        grid=(g_tot // gb,),
        in_specs=[seq_spec] * 6 + [st_spec] + [par_spec] * 5,
        out_specs=[seq_spec, st_spec],
        out_shape=[jax.ShapeDtypeStruct((g_tot, t, n), F32), jax.ShapeDtypeStruct((g_tot, n, n), F32)],
        compiler_params=_cparams(1),
        name="rwkv_scan_groups",
    )(*seqs, s0, *params)


def _dsa_prep_kernel(q_ref, k_ref, qg_ref, kg_ref, qo_ref, ko_ref, kb_ref):
    def head_norm(x, g):
        outs = []
        for h in range(x.shape[1] // ATT_HEAD_DIM):
            xh = x[:, h * ATT_HEAD_DIM:(h + 1) * ATT_HEAD_DIM]
            outs.append(xh * lax.rsqrt(jnp.mean(xh * xh, axis=-1, keepdims=True) + NORM_EPS) * g)
        return jnp.concatenate(outs, axis=1)

    qo_ref[...] = (head_norm(q_ref[...], qg_ref[...]) * QK_SCALE_LOG2).astype(qo_ref.dtype)
    kn = head_norm(k_ref[...], kg_ref[...])
    ko_ref[...] = kn
    kb_ref[...] = kn.astype(kb_ref.dtype)


def _dsa_prep(z, off, qg, kg, d, tm=256):
    m = z.shape[0]
    tm = min(tm, m)
    kv = KV_HEADS * ATT_HEAD_DIM
    return pl.pallas_call(
        _dsa_prep_kernel,
        grid=(m // tm,),
        in_specs=[pl.BlockSpec((tm, d), lambda i: (i, off["q"] // d)),
                  pl.BlockSpec((tm, kv), lambda i: (i, off["kk"] // kv)),
                  pl.BlockSpec((1, ATT_HEAD_DIM), lambda i: (0, 0)),
                  pl.BlockSpec((1, ATT_HEAD_DIM), lambda i: (0, 0))],
        out_specs=[pl.BlockSpec((tm, d), lambda i: (i, 0)),
                   pl.BlockSpec((tm, kv), lambda i: (i, 0)),
                   pl.BlockSpec((tm, kv), lambda i: (i, 0))],
        out_shape=[jax.ShapeDtypeStruct((m, d), BF16), jax.ShapeDtypeStruct((m, kv), F32),
                   jax.ShapeDtypeStruct((m, kv), BF16)],
        compiler_params=_cparams(1),
        name="dsa_prep",
    )(z, z, qg, kg)


def _sort_key(x):
    x = jnp.where(x == 0.0, 0.0, x)
    bits = pltpu.bitcast(x, I32)
    return bits ^ (lax.shift_right_arithmetic(bits, 31) & 0x7FFFFFFF)


def _topk_threshold(skey_ref, n_blocks, k_top, rows):
    def count(pred_fn):
        hit = lambda j: jnp.where(pred_fn(skey_ref[j]), 1.0, 0.0)
        if isinstance(n_blocks, int):
            acc = hit(0)
            for j in range(1, n_blocks):
                acc = acc + hit(j)
        else:
            body = lambda j2, acc: acc + (hit(2 * j2) + hit(2 * j2 + 1))
            acc = lax.fori_loop(0, (n_blocks + 1) // 2, body, jnp.zeros((rows, LANES), F32))
        return jnp.sum(acc, axis=1, keepdims=True)

    def bit_body(it, tau):
        cand = tau + lax.shift_left(jnp.int32(1), 31 - it)
        return jnp.where(count(lambda s: s >= cand) >= k_top, cand, tau)

    def two_bit_body(it, tau):
        hi = lax.shift_left(jnp.int32(1), 31 - 2 * it)
        lo = lax.shift_left(jnp.int32(1), 30 - 2 * it)
        c1, c2, c3 = tau + lo, tau + hi, tau + hi + lo
        n1, n2, n3 = (count(lambda s, c=c: s >= c) for c in (c1, c2, c3))
        return jnp.where(n3 >= k_top, c3, jnp.where(n2 >= k_top, c2, jnp.where(n1 >= k_top, c1, tau)))

    tau0 = jnp.full((rows, 1), INT_MIN, I32)
    if isinstance(n_blocks, int):
        tau = lax.fori_loop(0, 16, two_bit_body, tau0)
    else:
        tau = lax.fori_loop(0, 32, bit_body, tau0)
    need = k_top - count(lambda s: s > tau)
    return tau, need, count(lambda s: s == tau)


def _selected(skey, tau, need, eq_before, tri):
    eq = skey == tau
    eqf = jnp.where(eq, 1.0, 0.0)
    prefix = eq_before + jnp.dot(eqf.astype(BF16), tri, preferred_element_type=F32)
    sel = (skey > tau) | (eq & (prefix <= need))
    return sel, eq_before + jnp.sum(eqf, axis=1, keepdims=True)


def _tri_incl():
    return (lax.broadcasted_iota(I32, (LANES, LANES), 0) <= lax.broadcasted_iota(I32, (LANES, LANES), 1)).astype(BF16)


def _sel_prompt_kernel(qi_ref, wi_ref, ki_ref, bias_ref, skey_scr, qm_scr, wb_scr, *, k_top):
    i = pl.program_id(0)
    qb = qi_ref.shape[0]
    nkb = bias_ref.shape[1] // LANES
    kcols = lambda j: pl.ds(pl.multiple_of(j * LANES, LANES), LANES)
    lane = lax.broadcasted_iota(I32, (qb, LANES), 1)
    rowq = lax.broadcasted_iota(I32, (qb, LANES), 0)
    wi = wi_ref[...] * (IDX_HEADS ** -0.5)
    for hp in range(IDX_HEADS // 2):
        qpair = qi_ref[:, hp * LANES:(hp + 1) * LANES] * (IDX_DIM ** -0.5)
        qm_scr[2 * hp * qb:(2 * hp + 1) * qb, :] = jnp.where(lane < IDX_DIM, qpair, 0.0).astype(BF16)
        qm_scr[(2 * hp + 1) * qb:(2 * hp + 2) * qb, :] = jnp.where(lane >= IDX_DIM, qpair, 0.0).astype(BF16)
    for h in range(IDX_HEADS):
        wb_scr[h] = jnp.broadcast_to(wi[:, h:h + 1], (qb, LANES))

    def score_blocks(j2, carry):
        kblk = ki_ref[pl.ds(pl.multiple_of(j2 * 2 * LANES, 2 * LANES), 2 * LANES), :].astype(BF16)
        dots = lax.dot_general(qm_scr[...], kblk, (_NT, ((), ())), preferred_element_type=F32)
        for e in range(2):
            sc = jnp.zeros((qb, LANES), F32)
            for h in range(IDX_HEADS):
                sc = sc + jnp.maximum(dots[h * qb:(h + 1) * qb, e * LANES:(e + 1) * LANES], 0.0) * wb_scr[h]
            j = 2 * j2 + e
            allowed = (j < i) | ((j == i) & (lane <= rowq))
            skey_scr[j] = _sort_key(jnp.where(allowed, sc, NEG_INF))
        return carry

    lax.fori_loop(0, (i + 2) // 2, score_blocks, 0)
    tau, need, n_eq = _topk_threshold(skey_scr, i + 1, float(k_top), qb)

    def write_with_ties():
        tri = _tri_incl()

        def write_block(j, eq_before):
            sel, eq_after = _selected(skey_scr[j], tau, need, eq_before, tri)
            allowed = (j < i) | (lane <= rowq)
            bias_ref[:, kcols(j)] = jnp.where(sel & allowed, 0.0, NEG_INF).astype(bias_ref.dtype)
            return eq_after

        lax.fori_loop(0, i + 1, write_block, jnp.zeros((qb, 1), F32))

    def write_no_ties():
        def write_block(j, carry):
            allowed = (j < i) | (lane <= rowq)
            bias_ref[:, kcols(j)] = jnp.where((skey_scr[j] >= tau) & allowed, 0.0, NEG_INF).astype(bias_ref.dtype)
            return carry

        lax.fori_loop(0, i + 1, write_block, 0)

    surplus_ties = jnp.max(jnp.where(n_eq > need, 1.0, 0.0)) > 0.0
    lax.cond(surplus_ties, write_with_ties, write_no_ties)

    def fill_block(j, carry):
        bias_ref[:, kcols(j)] = jnp.full((qb, LANES), NEG_INF, bias_ref.dtype)
        return carry

    lax.fori_loop(i + 1, nkb, fill_block, 0)


def _sel_prompt(z, off, k_top):
    t = z.shape[0]
    qb = LANES
    nkb = t // LANES
    nqb = t // qb
    qi_w = IDX_HEADS * IDX_DIM
    return pl.pallas_call(
        functools.partial(_sel_prompt_kernel, k_top=k_top),
        grid=(nqb,),
        in_specs=[pl.BlockSpec((qb, qi_w), lambda i: (i, off["qi"] // qi_w)),
                  pl.BlockSpec((qb, LANES), lambda i: (i, off["wi"] // LANES)),
                  pl.BlockSpec((t, LANES), lambda i: (0, off["ki2"] // LANES))],
        out_specs=pl.BlockSpec((qb, t), lambda i: (i, 0)),
        out_shape=jax.ShapeDtypeStruct((t, t), BF16),
        scratch_shapes=[pltpu.VMEM((nkb, qb, LANES), I32),
                        pltpu.VMEM((IDX_HEADS * qb, LANES), BF16),
                        pltpu.VMEM((IDX_HEADS, qb, LANES), F32)],
        compiler_params=_cparams(1),
        name="dsa_select_prompt",
    )(z, z, z)


ATT_TILE = 256


def _att_prompt_kernel(qidx_ref, kidx_ref, q_ref, k_ref, v_ref, bias_ref, o_ref, m_scr, l_scr, acc_scr):
    p = pl.program_id(0)
    qi = qidx_ref[p]
    kj = kidx_ref[p]
    tq = q_ref.shape[0]
    n_heads = q_ref.shape[1] // ATT_HEAD_DIM
    group = n_heads // KV_HEADS
    @pl.when(kj == 0)
    def _():
        m_scr[...] = jnp.full_like(m_scr, NEG_INF)
        l_scr[...] = jnp.zeros_like(l_scr)
        acc_scr[...] = jnp.zeros_like(acc_scr)

    bias = bias_ref[...].astype(F32)
    bias = jnp.concatenate([bias] * group, axis=0)
    ones = jnp.ones((k_ref.shape[0], LANES), BF16)
    for n in range(KV_HEADS):
        cols = lambda h: slice(h * ATT_HEAD_DIM, (h + 1) * ATT_HEAD_DIM)
        q = jnp.concatenate([q_ref[:, cols(n * group + g)] for g in range(group)], axis=0)
        s = lax.dot_general(q, k_ref[:, cols(n)], (_NT, ((), ())), preferred_element_type=F32) + bias
        m_prev = m_scr[n]
        m_cur = jnp.maximum(m_prev, jnp.max(s, axis=1, keepdims=True))
        m_safe = jnp.where(m_cur == NEG_INF, 0.0, m_cur)
        pexp = jnp.exp2(s - jnp.concatenate([m_safe] * (s.shape[1] // LANES), axis=1))
        alpha = jnp.exp2(m_prev - m_safe)
        pv = jnp.dot(pexp.astype(BF16), jnp.concatenate([v_ref[:, cols(n)], ones], axis=1),
                     preferred_element_type=F32)
        acc_scr[n] = alpha * acc_scr[n] + pv[:, :ATT_HEAD_DIM]
        l_scr[n] = alpha * l_scr[n] + pv[:, ATT_HEAD_DIM:]
        m_scr[n] = m_cur

    @pl.when(kj == qi)
    def _():
        for n in range(KV_HEADS):
            res = acc_scr[n] / l_scr[n]
            for g in range(group):
                h = n * group + g
                o_ref[:, h * ATT_HEAD_DIM:(h + 1) * ATT_HEAD_DIM] = res[g * tq:(g + 1) * tq].astype(o_ref.dtype)


def _att_prompt(q, kb, vb, bias):
    t, d = q.shape
    tile = min(ATT_TILE, t)
    kv = kb.shape[1]
    nqb = t // tile
    pairs = [(i, j) for i in range(nqb) for j in range(i + 1)]
    qidx = jnp.asarray(np.array([p[0] for p in pairs], np.int32))
    kidx = jnp.asarray(np.array([p[1] for p in pairs], np.int32))
    group = d // ATT_HEAD_DIM // KV_HEADS
    grid_spec = pltpu.PrefetchScalarGridSpec(
        num_scalar_prefetch=2,
        grid=(len(pairs),),
        in_specs=[pl.BlockSpec((tile, d), lambda p, qi, kj: (qi[p], 0)),
                  pl.BlockSpec((tile, kv), lambda p, qi, kj: (kj[p], 0)),
                  pl.BlockSpec((tile, kv), lambda p, qi, kj: (kj[p], 0)),
                  pl.BlockSpec((tile, tile), lambda p, qi, kj: (qi[p], kj[p]))],
        out_specs=pl.BlockSpec((tile, d), lambda p, qi, kj: (qi[p], 0)),
        scratch_shapes=[pltpu.VMEM((KV_HEADS, group * tile, ATT_HEAD_DIM), F32)] * 3,
    )
    return pl.pallas_call(
        _att_prompt_kernel,
        grid_spec=grid_spec,
        out_shape=jax.ShapeDtypeStruct((t, d), BF16),
        compiler_params=_cparams(1),
        name="dsa_attn_prompt",
    )(qidx, kidx, q, kb, vb, bias)


def _dsa_sample_kernel(pt_ref, qall_ref, wcol_ref, qg_ref, knew_i_ref, knew_ref, vnew_ref, *rest, n_pages, k_top):
    ki_pages = rest[:n_pages]
    k_pages = rest[n_pages:2 * n_pages]
    v_pages = rest[2 * n_pages:3 * n_pages]
    o_ref, skey_scr, k_scr, v_scr = rest[3 * n_pages:]
    t_new = knew_ref.shape[1]
    rows = qall_ref.shape[1]
    group_rows = qg_ref.shape[2]
    lane = lax.broadcasted_iota(I32, (t_new, LANES), 1)
    rowq = lax.broadcasted_iota(I32, (t_new, LANES), 0)
    new_ok = lane <= rowq

    def pad_page(x):
        return jnp.concatenate([x, jnp.zeros((PAGE_SIZE - t_new, x.shape[1]), x.dtype)], axis=0)

    qa = qall_ref[0].astype(BF16)
    wc = wcol_ref[0] * (IDX_HEADS ** -0.5)

    def block_scores(kblk, contract):
        dots = lax.dot_general(qa, kblk.astype(BF16), (contract, ((), ())), preferred_element_type=F32)
        part = jnp.maximum(dots * (IDX_DIM ** -0.5), 0.0) * wc
        sc = part[0:t_new]
        for h in range(1, rows // t_new):
            sc = sc + part[h * t_new:(h + 1) * t_new]
        return sc

    head_rows = lambda n: pl.ds(n, PAGE_SIZE, stride=KV_HEADS)
    head_cols = lambda n: slice(n * ATT_HEAD_DIM, (n + 1) * ATT_HEAD_DIM)
    for p in range(n_pages):
        skey_scr[p] = _sort_key(block_scores(ki_pages[p][0], _NN))
        for n in range(KV_HEADS):
            k_scr[n, p * PAGE_SIZE:(p + 1) * PAGE_SIZE, :] = k_pages[p][head_rows(n), :].astype(BF16)
            v_scr[n, p * PAGE_SIZE:(p + 1) * PAGE_SIZE, :] = v_pages[p][head_rows(n), :].astype(BF16)
    skey_scr[n_pages] = _sort_key(jnp.where(new_ok, block_scores(pad_page(knew_i_ref[0]), _NT), NEG_INF))
    knew = pad_page(knew_ref[0]).astype(BF16)
    vnew = pad_page(vnew_ref[0]).astype(BF16)
    for n in range(KV_HEADS):
        k_scr[n, n_pages * PAGE_SIZE:(n_pages + 1) * PAGE_SIZE, :] = knew[:, head_cols(n)]
        v_scr[n, n_pages * PAGE_SIZE:(n_pages + 1) * PAGE_SIZE, :] = vnew[:, head_cols(n)]

    tau, need, _ = _topk_threshold(skey_scr, n_pages + 1, float(k_top), t_new)
    tri = _tri_incl()
    eq_before = jnp.zeros((t_new, 1), F32)
    bias = []
    for p in range(n_pages + 1):
        sel, eq_before = _selected(skey_scr[p], tau, need, eq_before, tri)
        ok = sel if p < n_pages else sel & new_ok
        bias.append(jnp.where(ok, 0.0, NEG_INF))
    bias = jnp.concatenate(bias, axis=1)
    bias = jnp.concatenate([bias] * (group_rows // t_new), axis=0)

    for n in range(KV_HEADS):
        s = lax.dot_general(qg_ref[0, n], k_scr[n], (_NT, ((), ())), preferred_element_type=F32) + bias
        pexp = jnp.exp2(s - jnp.max(s, axis=1, keepdims=True))
        acc = jnp.dot(pexp.astype(BF16), v_scr[n], preferred_element_type=F32)
        o_ref[0, n] = acc / jnp.sum(pexp, axis=1, keepdims=True)


def _dsa_sample(page_table, qall, wcol, qg, knew_i, knew, vnew, cache_kidx, cache_k, cache_v, k_top):
    b, n_pages = page_table.shape
    t_new = knew.shape[1]
    kv = knew.shape[2]
    rows = qall.shape[1]
    group_rows = qg.shape[2]
    per_seq = lambda shape: pl.BlockSpec((1,) + shape, lambda bi, pt: (bi,) + (0,) * len(shape))
    idx_page = lambda p: pl.BlockSpec((1, IDX_DIM, PAGE_SIZE), lambda bi, pt: (pt[bi, p], 0, 0))
    kv_page = lambda p: pl.BlockSpec((PAGE_SIZE * KV_HEADS, ATT_HEAD_DIM), lambda bi, pt: (pt[bi, p], 0))
    grid_spec = pltpu.PrefetchScalarGridSpec(
        num_scalar_prefetch=1,
        grid=(b,),
        in_specs=[per_seq((rows, IDX_DIM)), per_seq((rows, 1)), per_seq((KV_HEADS, group_rows, ATT_HEAD_DIM)),
                  per_seq((t_new, IDX_DIM)), per_seq((t_new, kv)), per_seq((t_new, kv))]
                 + [idx_page(p) for p in range(n_pages)]
                 + [kv_page(p) for p in range(n_pages)] * 2,
        out_specs=per_seq((KV_HEADS, group_rows, ATT_HEAD_DIM)),
        scratch_shapes=[pltpu.VMEM((n_pages + 1, t_new, LANES), I32),
                        pltpu.VMEM((KV_HEADS, (n_pages + 1) * PAGE_SIZE, ATT_HEAD_DIM), BF16),
                        pltpu.VMEM((KV_HEADS, (n_pages + 1) * PAGE_SIZE, ATT_HEAD_DIM), BF16)],
    )
    return pl.pallas_call(
        functools.partial(_dsa_sample_kernel, n_pages=n_pages, k_top=k_top),
        grid_spec=grid_spec,
        out_shape=jax.ShapeDtypeStruct((b, KV_HEADS, group_rows, ATT_HEAD_DIM), F32),
        compiler_params=_cparams(1),
        name="dsa_sample",
    )(page_table, qall, wcol, qg, knew_i, knew, vnew,
      *([cache_kidx] * n_pages), *([cache_k] * n_pages), *([cache_v] * n_pages))


def _mix_kernel(xa_ref, xb_ref, wa_ref, wb_ref, ga_ref, gb_ref, o_ref):
    pa = jnp.dot(xa_ref[...], wa_ref[...], preferred_element_type=F32)
    pb = jnp.dot(xb_ref[...], wb_ref[...], preferred_element_type=F32)
    o_ref[...] = (_sigmoid(ga_ref[...]) * pa + _sigmoid(gb_ref[...]) * pb).astype(o_ref.dtype)


def _mix(o_rw, o_att, wa, wb, z, off, d, tm=512, tn=512):
    m = o_rw.shape[0]
    tm = min(tm, m)
    nb = d // tn
    return pl.pallas_call(
        _mix_kernel,
        grid=(m // tm, nb),
        in_specs=[pl.BlockSpec((tm, d), lambda i, j: (i, 0)),
                  pl.BlockSpec((tm, d), lambda i, j: (i, 0)),
                  pl.BlockSpec((d, tn), lambda i, j: (0, j)),
                  pl.BlockSpec((d, tn), lambda i, j: (0, j)),
                  pl.BlockSpec((tm, tn), lambda i, j: (i, off["ga"] // tn + j)),
                  pl.BlockSpec((tm, tn), lambda i, j: (i, off["gb"] // tn + j))],
        out_specs=pl.BlockSpec((tm, tn), lambda i, j: (i, j)),
        out_shape=jax.ShapeDtypeStruct((m, d), BF16),
        compiler_params=_cparams(2),
        name="branch_mix",
    )(o_rw, o_att, wa, wb, z, z)


def _conv_kernel(ug_ref, uv_ref, halo_ref, st0_ref, st1_ref, cw_ref, cb_ref, o_ref, *, seq_len):
    i = pl.program_id(0)
    x = ug_ref[...]
    acc = cb_ref[...] + x * cw_ref[CONV_W - 1:CONV_W, :]
    for back in range(1, CONV_W):
        prev = _prev_rows(x, halo_ref, [st0_ref, st1_ref], i, seq_len, back)
        acc = acc + prev * cw_ref[CONV_W - 1 - back:CONV_W - back, :]
    o_ref[...] = (acc * _sigmoid(acc) * uv_ref[...]).astype(o_ref.dtype)


def _conv_ffn(u, conv_prev, conv_w, conv_b, seq_len, d_ff, tm=512, tn=1408):
    m = u.shape[0]
    tm = min(tm, m)
    assert d_ff % tn == 0 and tn % LANES == 0
    nb = d_ff // tn
    assert conv_prev.shape[1] == CONV_W - 1 == 2
    if seq_len >= tm:
        st_spec = pl.BlockSpec((1, tn), lambda i, j: ((i * tm) // seq_len, j))
    else:
        st_spec = pl.BlockSpec((tm // seq_len, tn), lambda i, j: (i, j))
    return pl.pallas_call(
        functools.partial(_conv_kernel, seq_len=seq_len),
        grid=(m // tm, nb),
        in_specs=[pl.BlockSpec((tm, tn), lambda i, j: (i, j)),
                  pl.BlockSpec((tm, tn), lambda i, j: (i, nb + j)),
                  pl.BlockSpec((SUBLANES, tn), lambda i, j: (jnp.maximum(i * (tm // SUBLANES) - 1, 0), j)),
                  st_spec, st_spec,
                  pl.BlockSpec((CONV_W, tn), lambda i, j: (0, j)),
                  pl.BlockSpec((1, tn), lambda i, j: (0, j))],
        out_specs=pl.BlockSpec((tm, tn), lambda i, j: (i, j)),
        out_shape=jax.ShapeDtypeStruct((m, d_ff), BF16),
        compiler_params=_cparams(2),
        name="conv_ffn",
    )(u, u, u, conv_prev[:, 0], conv_prev[:, 1], conv_w, conv_b)


def _head_major(x, n_seq, seq_len, n_heads):
    return x.reshape(n_seq, seq_len, n_heads, RW_HEAD).transpose(0, 2, 1, 3).reshape(n_seq * n_heads, seq_len, RW_HEAD)


def _layer(x, mod, shift_prev, s0, conv_prev, W, cache):
    n_seq, seq_len, d = x.shape
    m = n_seq * seq_len
    off = W["off"]
    n_rw = d // RW_HEAD
    kv = KV_HEADS * ATT_HEAD_DIM
    xf = x.reshape(m, d)

    h = _norm_mod(xf, W["norm1_g"], mod, 1, 0, seq_len)
    z = _matmul(h, W["w_in"], tm=1024, tn=1024, name="in_proj")

    lora = lambda a: jnp.concatenate(
        [a[..., 3 * d:3 * d + DECAY_LORA], jnp.zeros(a.shape[:-1] + (LANES - DECAY_LORA,), a.dtype),
         a[..., 3 * d + DECAY_LORA:3 * d + DECAY_LORA + ICLR_LORA],
         jnp.zeros(a.shape[:-1] + (LANES - ICLR_LORA,), a.dtype),
         a[..., 3 * d + DECAY_LORA + ICLR_LORA:]], axis=-1)
    pack_rw = lambda a: jnp.concatenate([a[..., :3 * d], lora(a)], axis=-1)
    state_packed = pack_rw(shift_prev)
    r, k_raw, v_rw, decay, iclr, gate = _rwkv_prep(
        z, off, state_packed, W["mu_packed"], W["w0"], W["w_decay_up"], W["a0"], W["w_a_up"], W["w_g_up"],
        seq_len, d)
    if cache is None:
        assert n_seq == 1 and seq_len % SCAN_CHUNK == 0
        o_rw, s_pairs = _rwkv_scan_chunked((r, decay, k_raw, v_rw, iclr, gate), W["rw_rows"])
        s_new = jnp.stack([s_pairs[:, :RW_HEAD, :RW_HEAD], s_pairs[:, RW_HEAD:, RW_HEAD:]], axis=1).reshape(
            n_seq, n_rw, RW_HEAD, RW_HEAD)
    else:
        seqs = [_head_major(a, n_seq, seq_len, n_rw) for a in (r, decay, k_raw, v_rw, iclr, gate)]
        o_heads, s_new = _rwkv_scan_groups(seqs, s0.reshape(n_seq * n_rw, RW_HEAD, RW_HEAD), W["rw_params"])
        o_rw = o_heads.reshape(n_seq, n_rw, seq_len, RW_HEAD).transpose(0, 2, 1, 3).reshape(m, d).astype(BF16)
        s_new = s_new.reshape(n_seq, n_rw, RW_HEAD, RW_HEAD)
    z_last = z.reshape(n_seq, seq_len, z.shape[1])[:, -1]
    lora_cols = z_last[:, off["lora"]:off["lora"] + 512]
    shift_new = jnp.concatenate(
        [z_last[:, off["r"]:off["r"] + 3 * d],
         lora_cols[:, :DECAY_LORA], lora_cols[:, LANES:LANES + ICLR_LORA], lora_cols[:, 2 * LANES:]], axis=-1)

    qn, k_new, k_bf = _dsa_prep(z, off, W["q_norm_g"], W["k_norm_g"], d)
    v_new = z[:, off["vv"]:off["vv"] + kv]
    ki_new = z[:, off["ki2"]:off["ki2"] + IDX_DIM]
    if cache is None:
        assert n_seq == 1
        k_top = min(TOPK_MAX, seq_len // 4)
        bias = _sel_prompt(z, off, k_top)
        o_att = _att_prompt(qn, k_bf, v_new.astype(BF16), bias)
    else:
        cache_k, cache_v, cache_kidx, page_table = cache
        n_pages = page_table.shape[1]
        k_top = min(TOPK_MAX, (n_pages * PAGE_SIZE + seq_len) // 4)
        n_pool = cache_k.shape[0]
        qi = z[:, off["qi"]:off["qi"] + IDX_HEADS * IDX_DIM]
        qall = qi.reshape(n_seq, seq_len, IDX_HEADS, IDX_DIM).transpose(0, 2, 1, 3).reshape(
            n_seq, IDX_HEADS * seq_len, IDX_DIM)
        wi = z[:, off["wi"]:off["wi"] + IDX_HEADS]
        wcol = wi.reshape(n_seq, seq_len, IDX_HEADS).transpose(0, 2, 1).reshape(n_seq, IDX_HEADS * seq_len, 1)
        group = d // ATT_HEAD_DIM // KV_HEADS
        qg = qn.reshape(n_seq, seq_len, KV_HEADS, group, ATT_HEAD_DIM).transpose(0, 2, 3, 1, 4).reshape(
            n_seq, KV_HEADS, group * seq_len, ATT_HEAD_DIM)
        o_g = _dsa_sample(page_table, qall, wcol, qg, ki_new.reshape(n_seq, seq_len, IDX_DIM),
                          k_new.reshape(n_seq, seq_len, kv), v_new.reshape(n_seq, seq_len, kv),
                          jnp.swapaxes(cache_kidx, 1, 2),
                          cache_k.reshape(n_pool * PAGE_SIZE * KV_HEADS, ATT_HEAD_DIM),
                          cache_v.reshape(n_pool * PAGE_SIZE * KV_HEADS, ATT_HEAD_DIM), k_top)
        o_att = o_g.reshape(n_seq, KV_HEADS, group, seq_len, ATT_HEAD_DIM).transpose(0, 3, 1, 2, 4).reshape(
            m, d).astype(BF16)

    mixed = _mix(o_rw, o_att, W["w_proj_a"], W["w_proj_b"], z, off, d)
    x1 = _matmul(mixed, W["w_out"], tm=512, tn=512, res=xf, gate=mod, gate_col=2, seq_len=seq_len, name="out_proj")

    h2 = _norm_mod(x1, W["norm2_g"], mod, 4, 3, seq_len)
    u = _matmul(h2, W["w_ffn_up"], tm=1024, tn=1024, name="ffn_up")
    d_ff = W["w_ffn_down"].shape[0]
    act = _conv_ffn(u, conv_prev, W["conv_w"], W["conv_b"], seq_len, d_ff)
    x2 = _matmul(act, W["w_ffn_down"], tm=512, tn=512, res=x1, gate=mod, gate_col=5, seq_len=seq_len, name="ffn_down")
    assert seq_len >= CONV_W - 1
    conv_new = u.reshape(n_seq, seq_len, u.shape[1])[:, seq_len - (CONV_W - 1):, :d_ff]

    hd = lambda a, w: a.reshape(n_seq, seq_len, KV_HEADS, w)
    return (x2.reshape(n_seq, seq_len, d), hd(k_new, ATT_HEAD_DIM), hd(v_new, ATT_HEAD_DIM),
            ki_new.reshape(n_seq, seq_len, IDX_DIM), s_new, shift_new, conv_new)


def kernel(x_prompt, x_sample, c_prompt, c_sample, cache_k, cache_v, cache_kidx, page_table, state_rwkv,
           state_shift, state_conv, w_ada, b_ada, norm1_g, norm2_g, w_in, shift_mu, w0, w_decay_up, a0, w_a_up,
           w_g_up, k_k, k_a, r_k, lnx_w, lnx_b, q_norm_g, k_norm_g, w_proj_a, w_proj_b, w_out, w_ffn_up, conv_w,
           conv_b, w_ffn_down):
    depth = w_in.shape[0]
    d = x_prompt.shape[-1]
    bp, bs = x_prompt.shape[0], x_sample.shape[0]
    off, _ = _in_layout(d)
    n_rw = d // RW_HEAD
    yp, ys = x_prompt, x_sample
    st_p, st_s = [], []
    pad_rows = lambda a, rows: jnp.concatenate([a, jnp.zeros((rows - a.shape[0],) + a.shape[1:], a.dtype)], axis=0)
    for l in range(depth):
        heads = lambda a: a.reshape(n_rw, 1, RW_HEAD)
        mu = shift_mu[l]
        mu_lora = jnp.concatenate(
            [mu[3 * d:3 * d + DECAY_LORA], jnp.zeros((LANES - DECAY_LORA,), F32),
             mu[3 * d + DECAY_LORA:3 * d + DECAY_LORA + ICLR_LORA], jnp.zeros((LANES - ICLR_LORA,), F32),
             mu[3 * d + DECAY_LORA + ICLR_LORA:]])
        W = {
            "off": off,
            "norm1_g": norm1_g[l][None], "norm2_g": norm2_g[l][None],
            "w_in": _pack_cols(w_in[l], d),
            "mu_packed": jnp.concatenate([mu[:3 * d], mu_lora])[None],
            "w0": w0[l][None], "a0": a0[l][None],
            "w_decay_up": pad_rows(w_decay_up[l], LANES).astype(BF16),
            "w_a_up": pad_rows(w_a_up[l], LANES).astype(BF16),
            "w_g_up": w_g_up[l].astype(BF16),
            "rw_params": [heads(k_k[l]), heads(k_a[l]), heads(r_k[l]), heads(lnx_w[l]), heads(lnx_b[l])],
            "rw_rows": [a.reshape(1, d) for a in (k_k[l], k_a[l], r_k[l], lnx_w[l], lnx_b[l])],
            "q_norm_g": q_norm_g[l][None], "k_norm_g": k_norm_g[l][None],
            "w_proj_a": w_proj_a[l].astype(BF16), "w_proj_b": w_proj_b[l].astype(BF16),
            "w_out": w_out[l], "w_ffn_up": w_ffn_up[l],
            "conv_w": conv_w[l], "conv_b": conv_b[l][None], "w_ffn_down": w_ffn_down[l],
        }
        c_all = jnp.concatenate([c_prompt, c_sample], axis=0)
        rows = -(-c_all.shape[0] // SUBLANES) * SUBLANES
        mod = _matmul(pad_rows(c_all, rows), w_ada[l], tm=rows, tn=1024, bias=b_ada[l][None],
                      pre_silu=True, name="ada_mod")
        mod_p, mod_s = mod[:bp], mod[bp:bp + bs]
        d_ff = w_ffn_down.shape[1]
        yp, *sp = _layer(yp, mod_p, jnp.zeros((bp, state_shift.shape[-1]), F32),
                         jnp.zeros((bp, n_rw, RW_HEAD, RW_HEAD), F32), jnp.zeros((bp, CONV_W - 1, d_ff), F32),
                         W, None)
        ys, *ss = _layer(ys, mod_s, state_shift[l], state_rwkv[l], state_conv[l], W,
                         (cache_k[l], cache_v[l], cache_kidx[l], page_table))
        st_p.append(sp)
        st_s.append(ss)
    stk = lambda sts, i: sts[0][i][None] if len(sts) == 1 else jnp.stack([s[i] for s in sts], axis=0)
    return (yp, ys,
            stk(st_p, 0), stk(st_p, 1), stk(st_p, 2), stk(st_p, 3), stk(st_p, 4), stk(st_p, 5),
            stk(st_s, 0), stk(st_s, 1), stk(st_s, 2), stk(st_s, 3), stk(st_s, 4), stk(st_s, 5))
```

```python
import functools

import numpy as np
import jax
import jax.numpy as jnp
from jax import lax
from jax.experimental import pallas as pl
from jax.experimental.pallas import tpu as pltpu

F32 = jnp.float32
BF16 = jnp.bfloat16
I32 = jnp.int32

NORM_EPS = 1e-6
LNX_EPS = 64e-5
RW_HEAD = 64
ATT_HEAD_DIM = 128
KV_HEADS = 4
IDX_HEADS = 16
IDX_DIM = 64
PAGE_SIZE = 128
TOPK_MAX = 256
CONV_W = 3
DECAY_LORA = 96
ICLR_LORA = 96
GATE_LORA = 256

LANES = 128
SUBLANES = 8
VMEM_LIMIT = 56 * 1024 * 1024
NEG_INF = float("-inf")
INT_MIN = -2 ** 31
QK_SCALE_LOG2 =ATT_HEAD_DIM ** -0.5 * 1.4426950408889634


def _cparams(n_axes):
    return pltpu.CompilerParams(dimension_semantics=("arbitrary",) * n_axes,
                                vmem_limit_bytes=VMEM_LIMIT)


def _expand_rows(v, tm):
    s, w = v.shape
    if s == 1:
        return v
    return jnp.broadcast_to(v[:, None, :], (s, tm // s, w)).reshape(tm, w)


def _sigmoid(x):
    return 1.0 / (1.0 + jnp.exp(-x))


def _in_layout(d):
    rw = d
    att = d
    kv = KV_HEADS * ATT_HEAD_DIM
    qi = IDX_HEADS * IDX_DIM
    off = {}
    pos = 0
    for name, width in (("ga", d), ("gb", d), ("r", rw), ("k", rw), ("v", rw), ("q", att), ("qi", qi),
                        ("kk", kv), ("vv", kv), ("lora", 512), ("ki2", LANES), ("wi", LANES)):
        assert pos % width == 0, (name, pos, width)
        off[name] = pos
        pos += width
    total = -(-pos // 1024) * 1024
    return off, total


def _pack_rows(wt, d):
    rw, att = d, d
    kv = KV_HEADS * ATT_HEAD_DIM
    qi = IDX_HEADS * IDX_DIM
    sizes = (rw, rw, rw, DECAY_LORA, ICLR_LORA, GATE_LORA, att, kv, kv, qi, IDX_DIM, IDX_HEADS, d, d)
    cuts = np.cumsum((0,) + sizes)
    seg = {n: wt[cuts[i]:cuts[i + 1]] for i, n in enumerate(
        ("r", "k", "v", "wd", "ad", "gd", "q", "kk", "vv", "qi", "ki", "wi", "ga", "gb"))}
    z = lambda n: jnp.zeros((n,) + wt.shape[1:], wt.dtype)
    _, total = _in_layout(d)
    parts = [seg["ga"], seg["gb"], seg["r"], seg["k"], seg["v"], seg["q"], seg["qi"], seg["kk"], seg["vv"],
             seg["wd"], z(LANES - DECAY_LORA), seg["ad"], z(LANES - ICLR_LORA), seg["gd"],
             seg["ki"], seg["ki"], seg["wi"], z(LANES - IDX_HEADS)]
    used = sum(p.shape[0] for p in parts)
    parts.append(z(total - used))
    return jnp.concatenate(parts, axis=0)


def _mm_kernel(*refs, pre_silu, has_bias, has_res, w_transposed, precise=False):
    x_ref, w_ref = refs[0], refs[1]
    o_ref, wb_scr = refs[-2], refs[-1]

    @pl.when(pl.program_id(1) == 0)
    def _():
        w = w_ref[...]
        wb_scr[...] = (w.T if w_transposed else w).astype(wb_scr.dtype)

    x = x_ref[...]
    if pre_silu:
        x = x.astype(F32)
        x = x * _sigmoid(x)
    if precise:
        acc = jnp.dot(x, wb_scr[...], precision=lax.Precision.HIGHEST, preferred_element_type=F32)
    else:
        acc = jnp.dot(x.astype(BF16), wb_scr[...], preferred_element_type=F32)
    nxt = 2
    if has_bias:
        acc = acc + refs[nxt][...]
        nxt += 1
    if has_res:
        res_ref, gate_ref = refs[nxt], refs[nxt + 1]
        acc = res_ref[...] + _expand_rows(gate_ref[...], acc.shape[0]) * acc
    o_ref[...] = acc.astype(o_ref.dtype)


def _matmul(x, w, *, tm, tn, bias=None, pre_silu=False, res=None, gate=None, gate_col=0, seq_len=None,
            out_dtype=F32, w_transposed=False, precise=False, name="mm"):
    m, k = x.shape
    n = w.shape[0] if w_transposed else w.shape[1]
    tm = min(tm, m)
    assert m % tm == 0 and n % tn == 0, (m, tm, n, tn)
    w_spec = pl.BlockSpec((tn, k), lambda j, i: (j, 0)) if w_transposed else pl.BlockSpec((k, tn), lambda j, i: (0, j))
    in_specs = [pl.BlockSpec((tm, k), lambda j, i: (i, 0)), w_spec]
    args = [x, w]
    if bias is not None:
        in_specs.append(pl.BlockSpec((1, tn), lambda j, i: (0, j)))
        args.append(bias)
    if res is not None:
        in_specs.append(pl.BlockSpec((tm, tn), lambda j, i: (i, j)))
        args.append(res)
        gcol = gate_col * (n // tn)
        if seq_len >= tm:
            in_specs.append(pl.BlockSpec((1, tn), lambda j, i: ((i * tm) // seq_len, gcol + j)))
        else:
            in_specs.append(pl.BlockSpec((tm // seq_len, tn), lambda j, i: (i, gcol + j)))
        args.append(gate)
    return pl.pallas_call(
        functools.partial(_mm_kernel, pre_silu=pre_silu, has_bias=bias is not None, has_res=res is not None,
                          w_transposed=w_transposed, precise=precise),
        grid=(n // tn, m // tm),
        in_specs=in_specs,
        out_specs=pl.BlockSpec((tm, tn), lambda j, i: (i, j)),
        out_shape=jax.ShapeDtypeStruct((m, n), out_dtype),
        scratch_shapes=[pltpu.VMEM((k, tn), F32 if precise else BF16)],
        compiler_params=_cparams(2),
        name=name,
    )(*args)


def _norm_mod_kernel(x_ref, g_ref, sc_ref, sh_ref, o_ref):
    x = x_ref[...]
    tm = x.shape[0]
    y = x * lax.rsqrt(jnp.mean(x * x, axis=-1, keepdims=True) + NORM_EPS)
    y = y * g_ref[...]
    y = y * (1.0 + _expand_rows(sc_ref[...], tm)) + _expand_rows(sh_ref[...], tm)
    o_ref[...] = y.astype(o_ref.dtype)


def _mod_spec(tm, width, col, seq_len):
    if seq_len >= tm:
        return pl.BlockSpec((1, width), lambda i: ((i * tm) // seq_len, col))
    return pl.BlockSpec((tm // seq_len, width), lambda i: (i, col))


def _norm_mod(x, g, mod, sc_col, sh_col, seq_len, tm=512, out_dtype=BF16):
    m, d = x.shape
    tm = min(tm, m)
    return pl.pallas_call(
        _norm_mod_kernel,
        grid=(m // tm,),
        in_specs=[pl.BlockSpec((tm, d), lambda i: (i, 0)),
                  pl.BlockSpec((1, d), lambda i: (0, 0)),
                  _mod_spec(tm, d, sc_col, seq_len),
                  _mod_spec(tm, d, sh_col, seq_len)],
        out_specs=pl.BlockSpec((tm, d), lambda i: (i, 0)),
        out_shape=jax.ShapeDtypeStruct((m, d), out_dtype),
        compiler_params=_cparams(1),
        name="norm_mod",
    )(x, g, mod, mod)


def _prev_rows(x, halo_ref, state_refs, i, seq_len, shift):
    tm = x.shape[0]
    row = lax.broadcasted_iota(I32, (tm, 1), 0)
    out = pltpu.roll(x, shift, 0)
    n_state = len(state_refs)
    for r in range(shift):
        back = shift - r
        state_row = state_refs[n_state - back][...]
        if seq_len >= tm:
            src = jnp.where(i == 0, state_row, halo_ref[SUBLANES - back:SUBLANES - back + 1, :])
            out = jnp.where(row == r, src, out)
        else:
            out = jnp.where(row % seq_len == r, _expand_rows(state_row, tm), out)
    return out


def _halo_spec(tm, width, col):
    return pl.BlockSpec((SUBLANES, width), lambda i: (jnp.maximum(i * (tm // SUBLANES) - 1, 0), col))


def _state_spec(tm, width, col, seq_len):
    if seq_len >= tm:
        return pl.BlockSpec((1, width), lambda i: ((i * tm) // seq_len, col))
    return pl.BlockSpec((tm // seq_len, width), lambda i: (i, col))


def _softplus(x):
    return jnp.maximum(x, 0.0) + jnp.log1p(jnp.exp(-jnp.abs(x)))


def _rwkv_prep_kernel(zr_ref, zk_ref, zv_ref, zl_ref, hr_ref, hk_ref, hv_ref, hl_ref,
                      sr_ref, sk_ref, sv_ref, sl_ref, mur_ref, muk_ref, muv_ref, mul_ref,
                      w0_ref, wdec_ref, a0_ref, wa_ref, wg_ref,
                      r_ref, k_ref, v_ref, dec_ref, ic_ref, g_ref, *, seq_len):
    i = pl.program_id(0)

    def shifted(z_ref, h_ref, s_ref, mu_ref):
        x = z_ref[...]
        prev = _prev_rows(x, h_ref, [s_ref], i, seq_len, 1)
        return x + (prev - x) * mu_ref[...]

    r_ref[...] = shifted(zr_ref, hr_ref, sr_ref, mur_ref)
    k_ref[...] = shifted(zk_ref, hk_ref, sk_ref, muk_ref)
    v_ref[...] = shifted(zv_ref, hv_ref, sv_ref, muv_ref)
    zl = shifted(zl_ref, hl_ref, sl_ref, mul_ref)
    wd = zl[:, 0:LANES]
    ad = zl[:, LANES:2 * LANES]
    gd = zl[:, 2 * LANES:4 * LANES]
    mm = lambda a, w_ref: jnp.dot(a.astype(BF16), w_ref[...], preferred_element_type=F32)
    dec_in = w0_ref[...] + mm(jnp.tanh(wd), wdec_ref)
    w_log = -_softplus(-dec_in) - 0.5
    dec_ref[...] = -jnp.exp(w_log)
    ic_ref[...] = _sigmoid(a0_ref[...] + mm(ad, wa_ref))
    g_ref[...] = mm(_sigmoid(gd), wg_ref)


def _rwkv_prep(z, off, state_packed, mu_packed, w0, wdec, a0, wa, wg, seq_len, d, tm=256):
    m = z.shape[0]
    tm = min(tm, m)
    cb = lambda name, width: off[name] // width
    zspec = lambda name, width: pl.BlockSpec((tm, width), lambda i: (i, cb(name, width)))
    sspec = lambda c, width: _state_spec(tm, width, c, seq_len)
    mspec = lambda c, width: pl.BlockSpec((1, width), lambda i: (0, c))
    full = lambda a: pl.BlockSpec(a.shape, lambda i: (0,) * a.ndim)
    lcol = 3 * d // 512
    out_spec = pl.BlockSpec((tm, d), lambda i: (i, 0))
    out_sds = jax.ShapeDtypeStruct((m, d), F32)
    return pl.pallas_call(
        functools.partial(_rwkv_prep_kernel, seq_len=seq_len),
        grid=(m // tm,),
        in_specs=[zspec("r", d), zspec("k", d), zspec("v", d), zspec("lora", 512),
                  _halo_spec(tm, d, cb("r", d)), _halo_spec(tm, d, cb("k", d)), _halo_spec(tm, d, cb("v", d)),
                  _halo_spec(tm, 512, cb("lora", 512)),
                  sspec(0, d), sspec(1, d), sspec(2, d), sspec(lcol, 512),
                  mspec(0, d), mspec(1, d), mspec(2, d), mspec(lcol, 512),
                  full(w0), full(wdec), full(a0), full(wa), full(wg)],
        out_specs=[out_spec] * 6,
        out_shape=[out_sds] * 6,
        compiler_params=_cparams(1),
        name="rwkv_prep",
    )(z, z, z, z, z, z, z, z, state_packed, state_packed, state_packed, state_packed,
      mu_packed, mu_packed, mu_packed, mu_packed, w0, wdec, a0, wa, wg)


SCAN_CHUNK = 64


def _split_bf16(x):
    hi = x.astype(BF16)
    return hi, (x - hi.astype(F32)).astype(BF16)


def _dot3(a, b, contract):
    dg = lambda x, y: lax.dot_general(x, y, (contract, ((), ())), preferred_element_type=F32)
    return dg(a[0], b[0]) + (dg(a[0], b[1]) + dg(a[1], b[0]))


_NN = ((1,), (0,))
_NT = ((1,), (1,))
_TN = ((0,), (0,))


def _scan_chunk_kernel(r_ref, logw_ref, k_ref, v_ref, ic_ref, g_ref, kk_ref, ka_ref, rk_ref, lw_ref, lb_ref,
                       o_ref, sout_ref, sb_scr, o_scr, km_scr):
    ti = pl.program_id(1)
    tb, width = r_ref.shape
    n_pairs = width // LANES
    n_chunks = tb // SCAN_CHUNK
    c = SCAN_CHUNK
    n = RW_HEAD

    @pl.when(ti == 0)
    def _():
        sb_scr[...] = jnp.zeros_like(sb_scr)

    lane = lax.broadcasted_iota(I32, (1, LANES), 1)
    head0 = lane < n
    rowc = lax.broadcasted_iota(I32, (c, 1), 0)
    bf = lambda x: x.astype(BF16)
    dg = lambda x, y, contract: lax.dot_general(x, y, (contract, ((), ())), preferred_element_type=F32)

    def hsum(x):
        s0 = jnp.sum(jnp.where(head0, x, 0.0), axis=1, keepdims=True)
        s1 = jnp.sum(jnp.where(head0, 0.0, x), axis=1, keepdims=True)
        return jnp.where(head0, s0, s1)

    def stack(x):
        return jnp.concatenate([jnp.where(head0, x, 0.0), jnp.where(head0, 0.0, x)], axis=0)

    rl = lax.broadcasted_iota(I32, (2 * c, 2 * c), 0) % c
    cl = lax.broadcasted_iota(I32, (2 * c, 2 * c), 1) % c
    strict = cl < rl
    incl = cl <= rl

    def prepare(ci, pi):
        sl = slice(ci * c, (ci + 1) * c)
        cols = slice(pi * LANES, (pi + 1) * LANES)
        r, logw, kraw, v, ic = (ref[sl, cols] for ref in (r_ref, logw_ref, k_ref, v_ref, ic_ref))
        kk = kraw * kk_ref[:, cols]
        kk = kk / jnp.maximum(jnp.sqrt(hsum(kk * kk)), 1e-12)
        kmod = kraw * (1.0 + (ic - 1.0) * ka_ref[:, cols])
        km_scr[sl, cols] = kmod
        cum = logw
        sh = 1
        while sh < c:
            cum = cum + jnp.where(rowc >= sh, pltpu.roll(cum, sh, 0), 0.0)
            sh *= 2
        p_inv = jnp.exp(-cum)
        a_t = -kk * jnp.exp(cum - logw)
        r_t = r * jnp.exp(cum)
        b_t = kk * ic * p_inv
        k_t = kmod * p_inv
        ar = _split_bf16(jnp.concatenate([stack(a_t), stack(r_t)], axis=0))
        bk = _split_bf16(jnp.concatenate([stack(b_t), stack(k_t)], axis=0))
        vst = stack(v)
        g4 = dg(ar[0], bk[0], _NT)
        a_ab = jnp.where(strict, g4[0:2 * c, 0:2 * c], 0.0)
        a_ak = jnp.where(strict, g4[0:2 * c, 2 * c:4 * c], 0.0)
        a_rb = jnp.where(incl, g4[2 * c:4 * c, 0:2 * c], 0.0)
        a_rk = jnp.where(incl, g4[2 * c:4 * c, 2 * c:4 * c], 0.0)
        vs = _split_bf16(vst)
        return dict(ar=ar, bk=bk, vs=vs, mpow=a_ab, tm1=a_ab, a_rb=bf(a_rb), akv=dg(bf(a_ak), vs[0], _NN),
                    rkv=dg(bf(a_rk), vs[0], _NN), p_end=jnp.exp(cum[c - 1:c, :]))

    pre = [[prepare(ci, pi) for pi in range(n_pairs)] for ci in range(n_chunks)]
    items = [p for row in pre for p in row]
    span = 2
    while span < c:
        for p in items:
            mb = bf(p["mpow"])
            p["mpow"] = dg(mb, mb, _NN)
        for p in items:
            p["tm1"] = p["tm1"] + p["mpow"] + dg(bf(p["tm1"]), bf(p["mpow"]), _NN)
        span *= 2
    for p in items:
        p["tm1"] = bf(p["tm1"])

    sbs = [sb_scr[pi] for pi in range(n_pairs)]
    for ci in range(n_chunks):
        for pi in range(n_pairs):
            p = pre[ci][pi]
            xr = _dot3(p["ar"], _split_bf16(sbs[pi]), _NT)
            y = xr[0:2 * c] + p["akv"]
            ust = y + dg(p["tm1"], bf(y), _NN)
            us = _split_bf16(ust)
            ost = xr[2 * c:4 * c] + dg(p["a_rb"], us[0], _NN) + p["rkv"]
            o_scr[ci * c:(ci + 1) * c, pi * LANES:(pi + 1) * LANES] = ost[0:c] + ost[c:2 * c]
            uv = (jnp.concatenate([us[0], p["vs"][0]], axis=0), jnp.concatenate([us[1], p["vs"][1]], axis=0))
            sbs[pi] = (sbs[pi] + _dot3(uv, p["bk"], _TN)) * p["p_end"]
    for pi in range(n_pairs):
        sb_scr[pi] = sbs[pi]
        sout_ref[pi] = sbs[pi]

    for pi in range(n_pairs):
        cols = slice(pi * LANES, (pi + 1) * LANES)
        o = o_scr[:, cols]
        mu = hsum(o) * (1.0 / n)
        var = hsum(jnp.square(o - mu)) * (1.0 / n)
        on = (o - mu) * lax.rsqrt(var + LNX_EPS) * lw_ref[:, cols] + lb_ref[:, cols]
        bonus = hsum(r_ref[:, cols] * km_scr[:, cols] * rk_ref[:, cols]) * v_ref[:, cols]
        o_ref[:, cols] = ((on + bonus) * g_ref[:, cols]).astype(o_ref.dtype)


SCAN_PAIRS = 2


def _rwkv_scan_chunked(seqs, params, tb=512):
    t_tot, d = seqs[0].shape
    tb = min(tb, t_tot)
    assert t_tot % tb == 0 and tb % SCAN_CHUNK == 0 and d % LANES == 0
    n_pairs = d // LANES
    width = SCAN_PAIRS * LANES
    assert d % width == 0
    seq_spec = pl.BlockSpec((tb, width), lambda hp, ti: (ti, hp))
    par_spec = pl.BlockSpec((1, width), lambda hp, ti: (0, hp))
    return pl.pallas_call(
        _scan_chunk_kernel,
        grid=(d // width, t_tot // tb),
        in_specs=[seq_spec] * 6 + [par_spec] * 5,
        out_specs=[seq_spec, pl.BlockSpec((SCAN_PAIRS, LANES, LANES), lambda hp, ti: (hp, 0, 0))],
        out_shape=[jax.ShapeDtypeStruct((t_tot, d), BF16), jax.ShapeDtypeStruct((n_pairs, LANES, LANES), F32)],
        scratch_shapes=[pltpu.VMEM((SCAN_PAIRS, LANES, LANES), F32), pltpu.VMEM((tb, width), F32),
                        pltpu.VMEM((tb, width), F32)],
        compiler_params=_cparams(2),
        name="rwkv_scan_chunked",
    )(*seqs, *params)


SCAN_GROUPS = 32


def _scan_groups_kernel(r_ref, logw_ref, k_ref, v_ref, ic_ref, g_ref, s0_ref, kk_ref, ka_ref, rk_ref, lw_ref, lb_ref,
                        o_ref, sout_ref):
    gb, t, n = r_ref.shape
    rows = gb * t
    bf = lambda x: x.astype(BF16)
    dg = lambda x, y, contract: lax.dot_general(x, y, (contract, ((), ())), preferred_element_type=F32)
    flat = lambda ref: ref[...].reshape(rows, n)
    per_row = lambda ref: jnp.broadcast_to(ref[...], (gb, t, n)).reshape(rows, n)
    r, logw, kraw, v, ic = flat(r_ref), flat(logw_ref), flat(k_ref), flat(v_ref), flat(ic_ref)
    kk = kraw * per_row(kk_ref)
    kk = kk / jnp.maximum(jnp.sqrt(jnp.sum(kk * kk, axis=1, keepdims=True)), 1e-12)
    kmod = kraw * (1.0 + (ic - 1.0) * per_row(ka_ref))
    pos = lax.broadcasted_iota(I32, (rows, 1), 0) % t
    cum = logw
    sh = 1
    while sh < t:
        cum = cum + jnp.where(pos >= sh, pltpu.roll(cum, sh, 0), 0.0)
        sh *= 2
    p_inv = jnp.exp(-cum)
    a_t = -kk * jnp.exp(cum - logw)
    r_t = r * jnp.exp(cum)
    b_t = kk * ic * p_inv
    k_t = kmod * p_inv
    p_end = jnp.exp(cum)

    ri = lax.broadcasted_iota(I32, (rows, rows), 0)
    ci = lax.broadcasted_iota(I32, (rows, rows), 1)
    same = (ri // t) == (ci // t)
    strict = same & (ci < ri)
    incl = same & (ci <= ri)
    g4 = dg(bf(jnp.concatenate([a_t, r_t], axis=0)), bf(jnp.concatenate([b_t, k_t], axis=0)), _NT)
    a_ab = jnp.where(strict, g4[0:rows, 0:rows], 0.0)
    a_ak = jnp.where(strict, g4[0:rows, rows:2 * rows], 0.0)
    a_rb = jnp.where(incl, g4[rows:2 * rows, 0:rows], 0.0)
    a_rk = jnp.where(incl, g4[rows:2 * rows, rows:2 * rows], 0.0)
    mpow = a_ab
    tm1 = a_ab
    span = 2
    while span < t:
        mb = bf(mpow)
        mpow = dg(mb, mb, _NN)
        tm1 = tm1 + mpow + dg(bf(tm1), bf(mpow), _NN)
        span *= 2
    vb = bf(v)
    akv = dg(bf(a_ak), vb, _NN)
    rkv = dg(bf(a_rk), vb, _NN)

    states = [s0_ref[gi] for gi in range(gb)]
    xs, rss = [], []
    for gi in range(gb):
        sl = slice(gi * t, (gi + 1) * t)
        ar = _split_bf16(jnp.concatenate([a_t[sl], r_t[sl]], axis=0))
        xr = _dot3(ar, _split_bf16(states[gi]), _NT)
        xs.append(xr[0:t])
        rss.append(xr[t:2 * t])
    y = jnp.concatenate(xs, axis=0) + akv
    ust = y + dg(bf(tm1), bf(y), _NN)
    o = jnp.concatenate(rss, axis=0) + dg(bf(a_rb), bf(ust), _NN) + rkv
    for gi in range(gb):
        sl = slice(gi * t, (gi + 1) * t)
        uv = _split_bf16(jnp.concatenate([ust[sl], v[sl]], axis=0))
        bk = _split_bf16(jnp.concatenate([b_t[sl], k_t[sl]], axis=0))
        sout_ref[gi] = (states[gi] + _dot3(uv, bk, _TN)) * p_end[(gi + 1) * t - 1:(gi + 1) * t, :]

    mu = jnp.mean(o, axis=1, keepdims=True)
    var = jnp.mean(jnp.square(o - mu), axis=1, keepdims=True)
    on = (o - mu) * lax.rsqrt(var + LNX_EPS) * per_row(lw_ref) + per_row(lb_ref)
    bonus = jnp.sum(r * kmod * per_row(rk_ref), axis=1, keepdims=True) * v
    o_ref[...] = ((on + bonus) * flat(g_ref)).reshape(gb, t, n)


def _rwkv_scan_groups(seqs, s0, params):
    g_tot, t, n = seqs[0].shape
    n_heads = params[0].shape[0]
    gb = SCAN_GROUPS
    assert g_tot % gb == 0 and n_heads % gb == 0 and t % SUBLANES == 0
    hblocks = n_heads // gb
    seq_spec = pl.BlockSpec((gb, t, n), lambda gi: (gi, 0, 0))
    st_spec = pl.BlockSpec((gb, n, n), lambda gi: (gi, 0, 0))
    par_spec = pl.BlockSpec((gb, 1, n), lambda gi: (gi % hblocks, 0, 0))
    return pl.pallas_call(
        _scan_groups_kernel,
        grid=(g_tot // gb,),
        in_specs=[seq_spec] * 6 + [st_spec] + [par_spec] * 5,
        out_specs=[seq_spec, st_spec],
        out_shape=[jax.ShapeDtypeStruct((g_tot, t, n), F32), jax.ShapeDtypeStruct((g_tot, n, n), F32)],
        compiler_params=_cparams(1),
        name="rwkv_scan_groups",
    )(*seqs, s0, *params)


def _dsa_prep_kernel(q_ref, k_ref, qg_ref, kg_ref, qo_ref, ko_ref, kb_ref):
    def head_norm(x, g):
        outs = []
        for h in range(x.shape[1] // ATT_HEAD_DIM):
            xh = x[:, h * ATT_HEAD_DIM:(h + 1) * ATT_HEAD_DIM]
            outs.append(xh * lax.rsqrt(jnp.mean(xh * xh, axis=-1, keepdims=True) + NORM_EPS) * g)
        return jnp.concatenate(outs, axis=1)

    qo_ref[...] = (head_norm(q_ref[...], qg_ref[...]) * QK_SCALE_LOG2).astype(qo_ref.dtype)
    kn = head_norm(k_ref[...], kg_ref[...])
    ko_ref[...] = kn
    kb_ref[...] = kn.astype(kb_ref.dtype)


def _dsa_prep(z, off, qg, kg, d, tm=512):
    m = z.shape[0]
    tm = min(tm, m)
    kv = KV_HEADS * ATT_HEAD_DIM
    return pl.pallas_call(
        _dsa_prep_kernel,
        grid=(m // tm,),
        in_specs=[pl.BlockSpec((tm, d), lambda i: (i, off["q"] // d)),
                  pl.BlockSpec((tm, kv), lambda i: (i, off["kk"] // kv)),
                  pl.BlockSpec((1, ATT_HEAD_DIM), lambda i: (0, 0)),
                  pl.BlockSpec((1, ATT_HEAD_DIM), lambda i: (0, 0))],
        out_specs=[pl.BlockSpec((tm, d), lambda i: (i, 0)),
                   pl.BlockSpec((tm, kv), lambda i: (i, 0)),
                   pl.BlockSpec((tm, kv), lambda i: (i, 0))],
        out_shape=[jax.ShapeDtypeStruct((m, d), BF16), jax.ShapeDtypeStruct((m, kv), F32),
                   jax.ShapeDtypeStruct((m, kv), BF16)],
        compiler_params=_cparams(1),
        name="dsa_prep",
    )(z, z, qg, kg)


def _sort_key(x):
    x = jnp.where(x == 0.0, 0.0, x)
    bits = pltpu.bitcast(x, I32)
    return bits ^ (lax.shift_right_arithmetic(bits, 31) & 0x7FFFFFFF)


def _topk_threshold(skey_ref, n_blocks, k_top, rows):
    def count(pred_fn):
        hit = lambda j: jnp.where(pred_fn(skey_ref[j]), 1.0, 0.0)
        if isinstance(n_blocks, int):
            acc = hit(0)
            for j in range(1, n_blocks):
                acc = acc + hit(j)
        else:
            body = lambda j2, acc: acc + (hit(2 * j2) + hit(2 * j2 + 1))
            acc = lax.fori_loop(0, (n_blocks + 1) // 2, body, jnp.zeros((rows, LANES), F32))
        return jnp.sum(acc, axis=1, keepdims=True)

    def bit_body(it, tau):
        cand = tau + lax.shift_left(jnp.int32(1), 31 - it)
        return jnp.where(count(lambda s: s >= cand) >= k_top, cand, tau)

    def two_bit_body(it, tau):
        hi = lax.shift_left(jnp.int32(1), 31 - 2 * it)
        lo = lax.shift_left(jnp.int32(1), 30 - 2 * it)
        c1, c2, c3 = tau + lo, tau + hi, tau + hi + lo
        n1, n2, n3 = (count(lambda s, c=c: s >= c) for c in (c1, c2, c3))
        return jnp.where(n3 >= k_top, c3, jnp.where(n2 >= k_top, c2, jnp.where(n1 >= k_top, c1, tau)))

    tau0 = jnp.full((rows, 1), INT_MIN, I32)
    if isinstance(n_blocks, int):
        tau = lax.fori_loop(0, 16, two_bit_body, tau0)
    else:
        tau = lax.fori_loop(0, 32, bit_body, tau0)
    need = k_top - count(lambda s: s > tau)
    return tau, need, count(lambda s: s == tau)


def _selected(skey, tau, need, eq_before, tri):
    eq = skey == tau
    eqf = jnp.where(eq, 1.0, 0.0)
    prefix = eq_before + jnp.dot(eqf.astype(BF16), tri, preferred_element_type=F32)
    sel = (skey > tau) | (eq & (prefix <= need))
    return sel, eq_before + jnp.sum(eqf, axis=1, keepdims=True)


def _tri_incl():
    return (lax.broadcasted_iota(I32, (LANES, LANES), 0) <= lax.broadcasted_iota(I32, (LANES, LANES), 1)).astype(BF16)


def _sel_prompt_kernel(qi_ref, wi_ref, ki_ref, bias_ref, skey_scr, qm_scr, wb_scr, *, k_top):
    i = pl.program_id(0)
    qb = qi_ref.shape[0]
    nkb = bias_ref.shape[1] // LANES
    kcols = lambda j: pl.ds(pl.multiple_of(j * LANES, LANES), LANES)
    lane = lax.broadcasted_iota(I32, (qb, LANES), 1)
    rowq = lax.broadcasted_iota(I32, (qb, LANES), 0)
    wi = wi_ref[...] * (IDX_HEADS ** -0.5)
    for hp in range(IDX_HEADS // 2):
        qpair = qi_ref[:, hp * LANES:(hp + 1) * LANES] * (IDX_DIM ** -0.5)
        qm_scr[2 * hp * qb:(2 * hp + 1) * qb, :] = jnp.where(lane < IDX_DIM, qpair, 0.0)
        qm_scr[(2 * hp + 1) * qb:(2 * hp + 2) * qb, :] = jnp.where(lane >= IDX_DIM, qpair, 0.0)
    for h in range(IDX_HEADS):
        wb_scr[h] = jnp.broadcast_to(wi[:, h:h + 1], (qb, LANES))

    def score_blocks(j2, carry):
        kblk = ki_ref[pl.ds(pl.multiple_of(j2 * 2 * LANES, 2 * LANES), 2 * LANES), :]
        dots = lax.dot_general(qm_scr[...], kblk, (_NT, ((), ())), precision=lax.Precision.HIGHEST,
                               preferred_element_type=F32)
        for e in range(2):
            sc = jnp.zeros((qb, LANES), F32)
            for h in range(IDX_HEADS):
                sc = sc + jnp.maximum(dots[h * qb:(h + 1) * qb, e * LANES:(e + 1) * LANES], 0.0) * wb_scr[h]
            j = 2 * j2 + e
            allowed = (j < i) | ((j == i) & (lane <= rowq))
            skey_scr[j] = _sort_key(jnp.where(allowed, sc, NEG_INF))
        return carry

    lax.fori_loop(0, (i + 2) // 2, score_blocks, 0)
    tau, need, n_eq = _topk_threshold(skey_scr, i + 1, float(k_top), qb)

    def write_with_ties():
        tri = _tri_incl()

        def write_block(j, eq_before):
            sel, eq_after = _selected(skey_scr[j], tau, need, eq_before, tri)
            allowed = (j < i) | (lane <= rowq)
            bias_ref[:, kcols(j)] = jnp.where(sel & allowed, 0.0, NEG_INF).astype(bias_ref.dtype)
            return eq_after

        lax.fori_loop(0, i + 1, write_block, jnp.zeros((qb, 1), F32))

    def write_no_ties():
        def write_block(j, carry):
            allowed = (j < i) | (lane <= rowq)
            bias_ref[:, kcols(j)] = jnp.where((skey_scr[j] >= tau) & allowed, 0.0, NEG_INF).astype(bias_ref.dtype)
            return carry

        lax.fori_loop(0, i + 1, write_block, 0)

    surplus_ties = jnp.max(jnp.where(n_eq > need, 1.0, 0.0)) > 0.0
    lax.cond(surplus_ties, write_with_ties, write_no_ties)

    def fill_block(j, carry):
        bias_ref[:, kcols(j)] = jnp.full((qb, LANES), NEG_INF, bias_ref.dtype)
        return carry

    lax.fori_loop(i + 1, nkb, fill_block, 0)


def _sel_prompt(z, off, k_top):
    t = z.shape[0]
    qb = LANES
    nkb = t // LANES
    nqb = t // qb
    qi_w = IDX_HEADS * IDX_DIM
    return pl.pallas_call(
        functools.partial(_sel_prompt_kernel, k_top=k_top),
        grid=(nqb,),
        in_specs=[pl.BlockSpec((qb, qi_w), lambda i: (i, off["qi"] // qi_w)),
                  pl.BlockSpec((qb, LANES), lambda i: (i, off["wi"] // LANES)),
                  pl.BlockSpec((t, LANES), lambda i: (0, off["ki2"] // LANES))],
        out_specs=pl.BlockSpec((qb, t), lambda i: (i, 0)),
        out_shape=jax.ShapeDtypeStruct((t, t), BF16),
        scratch_shapes=[pltpu.VMEM((nkb, qb, LANES), I32),
                        pltpu.VMEM((IDX_HEADS * qb, LANES), F32),
                        pltpu.VMEM((IDX_HEADS, qb, LANES), F32)],
        compiler_params=_cparams(1),
        name="dsa_select_prompt",
    )(z, z, z)


ATT_Q_TILE = 256
ATT_K_TILE = 512


def _att_prompt_kernel(qidx_ref, kidx_ref, q_ref, k_ref, v_ref, bias_ref, o_ref, m_scr, l_scr, acc_scr):
    p = pl.program_id(0)
    qi = qidx_ref[p]
    kj = kidx_ref[p]
    tq = q_ref.shape[0]
    last_kj = (qi * tq) // k_ref.shape[0]
    n_heads = q_ref.shape[1] // ATT_HEAD_DIM
    group = n_heads // KV_HEADS
    @pl.when(kj == 0)
    def _():
        m_scr[...] = jnp.full_like(m_scr, NEG_INF)
        l_scr[...] = jnp.zeros_like(l_scr)
        acc_scr[...] = jnp.zeros_like(acc_scr)

    bias = bias_ref[...].astype(F32)
    bias = jnp.concatenate([bias] * group, axis=0)
    ones = jnp.ones((k_ref.shape[0], LANES), BF16)
    for n in range(KV_HEADS):
        cols = lambda h: slice(h * ATT_HEAD_DIM, (h + 1) * ATT_HEAD_DIM)
        q = jnp.concatenate([q_ref[:, cols(n * group + g)] for g in range(group)], axis=0)
        s = lax.dot_general(q, k_ref[:, cols(n)], (_NT, ((), ())), preferred_element_type=F32) + bias
        m_prev = m_scr[n]
        m_cur = jnp.maximum(m_prev, jnp.max(s, axis=1, keepdims=True))
        m_safe = jnp.where(m_cur == NEG_INF, 0.0, m_cur)
        pexp = jnp.exp2(s - jnp.concatenate([m_safe] * (s.shape[1] // LANES), axis=1))
        alpha = jnp.exp2(m_prev - m_safe)
        pv = jnp.dot(pexp.astype(BF16), jnp.concatenate([v_ref[:, cols(n)], ones], axis=1),
                     preferred_element_type=F32)
        acc_scr[n] = alpha * acc_scr[n] + pv[:, :ATT_HEAD_DIM]
        l_scr[n] = alpha * l_scr[n] + pv[:, ATT_HEAD_DIM:]
        m_scr[n] = m_cur

    @pl.when(kj == last_kj)
    def _():
        for n in range(KV_HEADS):
            res = acc_scr[n] / l_scr[n]
            for g in range(group):
                h = n * group + g
                o_ref[:, h * ATT_HEAD_DIM:(h + 1) * ATT_HEAD_DIM] = res[g * tq:(g + 1) * tq].astype(o_ref.dtype)


def _att_prompt(q, kb, vb, bias):
    t, d = q.shape
    tq = min(ATT_Q_TILE, t)
    tk = min(ATT_K_TILE, t)
    assert t % tq == 0 and t % tk == 0 and tk % tq == 0
    kv = kb.shape[1]
    pairs = [(i, j) for i in range(t // tq) for j in range((i * tq) // tk + 1)]
    qidx = jnp.asarray(np.array([p[0] for p in pairs], np.int32))
    kidx = jnp.asarray(np.array([p[1] for p in pairs], np.int32))
    group = d // ATT_HEAD_DIM // KV_HEADS
    grid_spec = pltpu.PrefetchScalarGridSpec(
        num_scalar_prefetch=2,
        grid=(len(pairs),),
        in_specs=[pl.BlockSpec((tq, d), lambda p, qi, kj: (qi[p], 0)),
                  pl.BlockSpec((tk, kv), lambda p, qi, kj: (kj[p], 0)),
                  pl.BlockSpec((tk, kv), lambda p, qi, kj: (kj[p], 0)),
                  pl.BlockSpec((tq, tk), lambda p, qi, kj: (qi[p], kj[p]))],
        out_specs=pl.BlockSpec((tq, d), lambda p, qi, kj: (qi[p], 0)),
        scratch_shapes=[pltpu.VMEM((KV_HEADS, group * tq, ATT_HEAD_DIM), F32)] * 3,
    )
    return pl.pallas_call(
        _att_prompt_kernel,
        grid_spec=grid_spec,
        out_shape=jax.ShapeDtypeStruct((t, d), BF16),
        compiler_params=_cparams(1),
        name="dsa_attn_prompt",
    )(qidx, kidx, q, kb, vb, bias)


def _dsa_sample_kernel(pt_ref, qall_ref, wcol_ref, qg_ref, knew_i_ref, knew_ref, vnew_ref, *rest, n_pages, k_top):
    ki_pages = rest[:n_pages]
    k_pages = rest[n_pages:2 * n_pages]
    v_pages = rest[2 * n_pages:3 * n_pages]
    o_ref, skey_scr, k_scr, v_scr = rest[3 * n_pages:]
    t_new = knew_ref.shape[1]
    rows = qall_ref.shape[1]
    group_rows = qg_ref.shape[2]
    lane = lax.broadcasted_iota(I32, (t_new, LANES), 1)
    rowq = lax.broadcasted_iota(I32, (t_new, LANES), 0)
    new_ok = lane <= rowq

    def pad_page(x):
        return jnp.concatenate([x, jnp.zeros((PAGE_SIZE - t_new, x.shape[1]), x.dtype)], axis=0)

    qa = qall_ref[0]
    wc = wcol_ref[0] * (IDX_HEADS ** -0.5)

    def block_scores(kblk, contract):
        dots = lax.dot_general(qa, kblk, (contract, ((), ())), precision=lax.Precision.HIGHEST,
                               preferred_element_type=F32)
        part = jnp.maximum(dots * (IDX_DIM ** -0.5), 0.0) * wc
        sc = part[0:t_new]
        for h in range(1, rows // t_new):
            sc = sc + part[h * t_new:(h + 1) * t_new]
        return sc

    head_rows = lambda n: pl.ds(n, PAGE_SIZE, stride=KV_HEADS)
    head_cols = lambda n: slice(n * ATT_HEAD_DIM, (n + 1) * ATT_HEAD_DIM)
    for p in range(n_pages):
        skey_scr[p] = _sort_key(block_scores(ki_pages[p][0], _NN))
        for n in range(KV_HEADS):
            k_scr[n, p * PAGE_SIZE:(p + 1) * PAGE_SIZE, :] = k_pages[p][head_rows(n), :].astype(BF16)
            v_scr[n, p * PAGE_SIZE:(p + 1) * PAGE_SIZE, :] = v_pages[p][head_rows(n), :].astype(BF16)
    skey_scr[n_pages] = _sort_key(jnp.where(new_ok, block_scores(pad_page(knew_i_ref[0]), _NT), NEG_INF))
    knew = pad_page(knew_ref[0]).astype(BF16)
    vnew = pad_page(vnew_ref[0]).astype(BF16)
    for n in range(KV_HEADS):
        k_scr[n, n_pages * PAGE_SIZE:(n_pages + 1) * PAGE_SIZE, :] = knew[:, head_cols(n)]
        v_scr[n, n_pages * PAGE_SIZE:(n_pages + 1) * PAGE_SIZE, :] = vnew[:, head_cols(n)]

    tau, need, _ = _topk_threshold(skey_scr, n_pages + 1, float(k_top), t_new)
    tri = _tri_incl()
    eq_before = jnp.zeros((t_new, 1), F32)
    bias = []
    for p in range(n_pages + 1):
        sel, eq_before = _selected(skey_scr[p], tau, need, eq_before, tri)
        ok = sel if p < n_pages else sel & new_ok
        bias.append(jnp.where(ok, 0.0, NEG_INF))
    bias = jnp.concatenate(bias, axis=1)
    bias = jnp.concatenate([bias] * (group_rows // t_new), axis=0)

    for n in range(KV_HEADS):
        s = lax.dot_general(qg_ref[0, n], k_scr[n], (_NT, ((), ())), preferred_element_type=F32) + bias
        pexp = jnp.exp2(s - jnp.max(s, axis=1, keepdims=True))
        acc = jnp.dot(pexp.astype(BF16), v_scr[n], preferred_element_type=F32)
        o_ref[0, n] = acc / jnp.sum(pexp, axis=1, keepdims=True)


def _dsa_sample(page_table, qall, wcol, qg, knew_i, knew, vnew, cache_kidx, cache_k, cache_v, k_top):
    b, n_pages = page_table.shape
    t_new = knew.shape[1]
    kv = knew.shape[2]
    rows = qall.shape[1]
    group_rows = qg.shape[2]
    per_seq = lambda shape: pl.BlockSpec((1,) + shape, lambda bi, pt: (bi,) + (0,) * len(shape))
    idx_page = lambda p: pl.BlockSpec((1, IDX_DIM, PAGE_SIZE), lambda bi, pt: (pt[bi, p], 0, 0))
    kv_page = lambda p: pl.BlockSpec((PAGE_SIZE * KV_HEADS, ATT_HEAD_DIM), lambda bi, pt: (pt[bi, p], 0))
    grid_spec = pltpu.PrefetchScalarGridSpec(
        num_scalar_prefetch=1,
        grid=(b,),
        in_specs=[per_seq((rows, IDX_DIM)), per_seq((rows, 1)), per_seq((KV_HEADS, group_rows, ATT_HEAD_DIM)),
                  per_seq((t_new, IDX_DIM)), per_seq((t_new, kv)), per_seq((t_new, kv))]
                 + [idx_page(p) for p in range(n_pages)]
                 + [kv_page(p) for p in range(n_pages)] * 2,
        out_specs=per_seq((KV_HEADS, group_rows, ATT_HEAD_DIM)),
        scratch_shapes=[pltpu.VMEM((n_pages + 1, t_new, LANES), I32),
                        pltpu.VMEM((KV_HEADS, (n_pages + 1) * PAGE_SIZE, ATT_HEAD_DIM), BF16),
                        pltpu.VMEM((KV_HEADS, (n_pages + 1) * PAGE_SIZE, ATT_HEAD_DIM), BF16)],
    )
    return pl.pallas_call(
        functools.partial(_dsa_sample_kernel, n_pages=n_pages, k_top=k_top),
        grid_spec=grid_spec,
        out_shape=jax.ShapeDtypeStruct((b, KV_HEADS, group_rows, ATT_HEAD_DIM), F32),
        compiler_params=_cparams(1),
        name="dsa_sample",
    )(page_table, qall, wcol, qg, knew_i, knew, vnew,
      *([cache_kidx] * n_pages), *([cache_k] * n_pages), *([cache_v] * n_pages))


def _mix_kernel(xa_ref, xb_ref, wa_ref, wb_ref, ga_ref, gb_ref, o_ref):
    pa = jnp.dot(xa_ref[...], wa_ref[...], preferred_element_type=F32)
    pb = jnp.dot(xb_ref[...], wb_ref[...], preferred_element_type=F32)
    o_ref[...] = (_sigmoid(ga_ref[...]) * pa + _sigmoid(gb_ref[...]) * pb).astype(o_ref.dtype)


def _mix(o_rw, o_att, wa, wb, z, off, d, tm=512, tn=512):
    m = o_rw.shape[0]
    tm = min(tm, m)
    nb = d // tn
    return pl.pallas_call(
        _mix_kernel,
        grid=(m // tm, nb),
        in_specs=[pl.BlockSpec((tm, d), lambda i, j: (i, 0)),
                  pl.BlockSpec((tm, d), lambda i, j: (i, 0)),
                  pl.BlockSpec((d, tn), lambda i, j: (0, j)),
                  pl.BlockSpec((d, tn), lambda i, j: (0, j)),
                  pl.BlockSpec((tm, tn), lambda i, j: (i, off["ga"] // tn + j)),
                  pl.BlockSpec((tm, tn), lambda i, j: (i, off["gb"] // tn + j))],
        out_specs=pl.BlockSpec((tm, tn), lambda i, j: (i, j)),
        out_shape=jax.ShapeDtypeStruct((m, d), BF16),
        compiler_params=_cparams(2),
        name="branch_mix",
    )(o_rw, o_att, wa, wb, z, z)


def _conv_kernel(ug_ref, uv_ref, halo_ref, st0_ref, st1_ref, cw_ref, cb_ref, o_ref, *, seq_len):
    i = pl.program_id(0)
    x = ug_ref[...]
    acc = cb_ref[...] + x * cw_ref[CONV_W - 1:CONV_W, :]
    for back in range(1, CONV_W):
        prev = _prev_rows(x, halo_ref, [st0_ref, st1_ref], i, seq_len, back)
        acc = acc + prev * cw_ref[CONV_W - 1 - back:CONV_W - back, :]
    o_ref[...] = (acc * _sigmoid(acc) * uv_ref[...]).astype(o_ref.dtype)


def _conv_ffn(u, conv_prev, conv_w, conv_b, seq_len, d_ff, tm=512, tn=1408):
    m = u.shape[0]
    tm = min(tm, m)
    assert d_ff % tn == 0 and tn % LANES == 0
    nb = d_ff // tn
    assert conv_prev.shape[1] == CONV_W - 1 == 2
    if seq_len >= tm:
        st_spec = pl.BlockSpec((1, tn), lambda i, j: ((i * tm) // seq_len, j))
    else:
        st_spec = pl.BlockSpec((tm // seq_len, tn), lambda i, j: (i, j))
    return pl.pallas_call(
        functools.partial(_conv_kernel, seq_len=seq_len),
        grid=(m // tm, nb),
        in_specs=[pl.BlockSpec((tm, tn), lambda i, j: (i, j)),
                  pl.BlockSpec((tm, tn), lambda i, j: (i, nb + j)),
                  pl.BlockSpec((SUBLANES, tn), lambda i, j: (jnp.maximum(i * (tm // SUBLANES) - 1, 0), j)),
                  st_spec, st_spec,
                  pl.BlockSpec((CONV_W, tn), lambda i, j: (0, j)),
                  pl.BlockSpec((1, tn), lambda i, j: (0, j))],
        out_specs=pl.BlockSpec((tm, tn), lambda i, j: (i, j)),
        out_shape=jax.ShapeDtypeStruct((m, d_ff), BF16),
        compiler_params=_cparams(2),
        name="conv_ffn",
    )(u, u, u, conv_prev[:, 0], conv_prev[:, 1], conv_w, conv_b)


def _head_major(x, n_seq, seq_len, n_heads):
    return x.reshape(n_seq, seq_len, n_heads, RW_HEAD).transpose(0, 2, 1, 3).reshape(n_seq * n_heads, seq_len, RW_HEAD)


def _layer(x, mod, shift_prev, s0, conv_prev, W, cache):
    n_seq, seq_len, d = x.shape
    m = n_seq * seq_len
    off = W["off"]
    n_rw = d // RW_HEAD
    kv = KV_HEADS * ATT_HEAD_DIM
    xf = x.reshape(m, d)

    h = _norm_mod(xf, W["norm1_g"], mod, 1, 0, seq_len)
    z = _matmul(h, W["w_in_t"], tm=1024, tn=1024, w_transposed=True, name="in_proj")
    h32 = _norm_mod(xf, W["norm1_g"], mod, 1, 0, seq_len, out_dtype=F32)
    w_idx_t = jnp.concatenate([W["w_in_t"][off["qi"]:off["qi"] + 1024], W["w_in_t"][off["ki2"]:off["wi"] + LANES]], axis=0)
    zi = _matmul(h32, w_idx_t, tm=512, tn=640, w_transposed=True, precise=True, name="in_proj_idx")
    offi = {"qi": 0, "ki2": 1024, "wi": 1152}

    lora = lambda a: jnp.concatenate(
        [a[..., 3 * d:3 * d + DECAY_LORA], jnp.zeros(a.shape[:-1] + (LANES - DECAY_LORA,), a.dtype),
         a[..., 3 * d + DECAY_LORA:3 * d + DECAY_LORA + ICLR_LORA],
         jnp.zeros(a.shape[:-1] + (LANES - ICLR_LORA,), a.dtype),
         a[..., 3 * d + DECAY_LORA + ICLR_LORA:]], axis=-1)
    pack_rw = lambda a: jnp.concatenate([a[..., :3 * d], lora(a)], axis=-1)
    state_packed = pack_rw(shift_prev)
    r, k_raw, v_rw, decay, iclr, gate = _rwkv_prep(
        z, off, state_packed, W["mu_packed"], W["w0"], W["w_decay_up"], W["a0"], W["w_a_up"], W["w_g_up"],
        seq_len, d)
    if cache is None:
        assert n_seq == 1 and seq_len % SCAN_CHUNK == 0
        o_rw, s_pairs = _rwkv_scan_chunked((r, decay, k_raw, v_rw, iclr, gate), W["rw_rows"])
        s_new = jnp.stack([s_pairs[:, :RW_HEAD, :RW_HEAD], s_pairs[:, RW_HEAD:, RW_HEAD:]], axis=1).reshape(
            n_seq, n_rw, RW_HEAD, RW_HEAD)
    else:
        seqs = [_head_major(a, n_seq, seq_len, n_rw) for a in (r, decay, k_raw, v_rw, iclr, gate)]
        o_heads, s_new = _rwkv_scan_groups(seqs, s0.reshape(n_seq * n_rw, RW_HEAD, RW_HEAD), W["rw_params"])
        o_rw = o_heads.reshape(n_seq, n_rw, seq_len, RW_HEAD).transpose(0, 2, 1, 3).reshape(m, d).astype(BF16)
        s_new = s_new.reshape(n_seq, n_rw, RW_HEAD, RW_HEAD)
    z_last = z.reshape(n_seq, seq_len, z.shape[1])[:, -1]
    lora_cols = z_last[:, off["lora"]:off["lora"] + 512]
    shift_new = jnp.concatenate(
        [z_last[:, off["r"]:off["r"] + 3 * d],
         lora_cols[:, :DECAY_LORA], lora_cols[:, LANES:LANES + ICLR_LORA], lora_cols[:, 2 * LANES:]], axis=-1)

    qn, k_new, k_bf = _dsa_prep(z, off, W["q_norm_g"], W["k_norm_g"], d)
    v_new = z[:, off["vv"]:off["vv"] + kv]
    ki_new = z[:, off["ki2"]:off["ki2"] + IDX_DIM]
    if cache is None:
        assert n_seq == 1
        k_top = min(TOPK_MAX, seq_len // 4)
        bias = _sel_prompt(zi, offi, k_top)
        o_att = _att_prompt(qn, k_bf, v_new.astype(BF16), bias)
    else:
        cache_k, cache_v, cache_kidx, page_table = cache
        n_pages = page_table.shape[1]
        k_top = min(TOPK_MAX, (n_pages * PAGE_SIZE + seq_len) // 4)
        n_pool = cache_k.shape[0]
        qi = zi[:, 0:IDX_HEADS * IDX_DIM]
        qall = qi.reshape(n_seq, seq_len, IDX_HEADS, IDX_DIM).transpose(0, 2, 1, 3).reshape(
            n_seq, IDX_HEADS * seq_len, IDX_DIM)
        wi = zi[:, 1152:1152 + IDX_HEADS]
        wcol = wi.reshape(n_seq, seq_len, IDX_HEADS).transpose(0, 2, 1).reshape(n_seq, IDX_HEADS * seq_len, 1)
        group = d // ATT_HEAD_DIM // KV_HEADS
        qg = qn.reshape(n_seq, seq_len, KV_HEADS, group, ATT_HEAD_DIM).transpose(0, 2, 3, 1, 4).reshape(
            n_seq, KV_HEADS, group * seq_len, ATT_HEAD_DIM)
        o_g = _dsa_sample(page_table, qall, wcol, qg, zi[:, 1024:1024 + IDX_DIM].reshape(n_seq, seq_len, IDX_DIM),
                          k_new.reshape(n_seq, seq_len, kv), v_new.reshape(n_seq, seq_len, kv),
                          jnp.swapaxes(cache_kidx, 1, 2),
                          cache_k.reshape(n_pool * PAGE_SIZE * KV_HEADS, ATT_HEAD_DIM),
                          cache_v.reshape(n_pool * PAGE_SIZE * KV_HEADS, ATT_HEAD_DIM), k_top)
        o_att = o_g.reshape(n_seq, KV_HEADS, group, seq_len, ATT_HEAD_DIM).transpose(0, 3, 1, 2, 4).reshape(
            m, d).astype(BF16)

    mixed = _mix(o_rw, o_att, W["w_proj_a"], W["w_proj_b"], z, off, d)
    x1 = _matmul(mixed, W["w_out"], tm=512, tn=512, res=xf, gate=mod, gate_col=2, seq_len=seq_len, name="out_proj")

    h2 = _norm_mod(x1, W["norm2_g"], mod, 4, 3, seq_len)
    u = _matmul(h2, W["w_ffn_up"], tm=1024, tn=1024, name="ffn_up")
    d_ff = W["w_ffn_down"].shape[0]
    act = _conv_ffn(u, conv_prev, W["conv_w"], W["conv_b"], seq_len, d_ff)
    x2 = _matmul(act, W["w_ffn_down"], tm=512, tn=512, res=x1, gate=mod, gate_col=5, seq_len=seq_len, name="ffn_down")
    assert seq_len >= CONV_W - 1
    conv_new = u.reshape(n_seq, seq_len, u.shape[1])[:, seq_len - (CONV_W - 1):, :d_ff]

    hd = lambda a, w: a.reshape(n_seq, seq_len, KV_HEADS, w)
    return (x2.reshape(n_seq, seq_len, d), hd(k_new, ATT_HEAD_DIM), hd(v_new, ATT_HEAD_DIM),
            ki_new.reshape(n_seq, seq_len, IDX_DIM), s_new, shift_new, conv_new)


def kernel(x_prompt, x_sample, c_prompt, c_sample, cache_k, cache_v, cache_kidx, page_table, state_rwkv,
           state_shift, state_conv, w_ada, b_ada, norm1_g, norm2_g, w_in, shift_mu, w0, w_decay_up, a0, w_a_up,
           w_g_up, k_k, k_a, r_k, lnx_w, lnx_b, q_norm_g, k_norm_g, w_proj_a, w_proj_b, w_out, w_ffn_up, conv_w,
           conv_b, w_ffn_down):
    depth = w_in.shape[0]
    d = x_prompt.shape[-1]
    bp, bs = x_prompt.shape[0], x_sample.shape[0]
    off, _ = _in_layout(d)
    n_rw = d // RW_HEAD
    yp, ys = x_prompt, x_sample
    st_p, st_s = [], []
    pad_rows = lambda a, rows: jnp.concatenate([a, jnp.zeros((rows - a.shape[0],) + a.shape[1:], a.dtype)], axis=0)
    for l in range(depth):
        heads = lambda a: a.reshape(n_rw, 1, RW_HEAD)
        mu = shift_mu[l]
        mu_lora = jnp.concatenate(
            [mu[3 * d:3 * d + DECAY_LORA], jnp.zeros((LANES - DECAY_LORA,), F32),
             mu[3 * d + DECAY_LORA:3 * d + DECAY_LORA + ICLR_LORA], jnp.zeros((LANES - ICLR_LORA,), F32),
             mu[3 * d + DECAY_LORA + ICLR_LORA:]])
        W = {
            "off": off,
            "norm1_g": norm1_g[l][None], "norm2_g": norm2_g[l][None],
            "w_in_t": _pack_rows(jnp.swapaxes(w_in[l], 0, 1), d),
            "mu_packed": jnp.concatenate([mu[:3 * d], mu_lora])[None],
            "w0": w0[l][None], "a0": a0[l][None],
            "w_decay_up": pad_rows(w_decay_up[l], LANES).astype(BF16),
            "w_a_up": pad_rows(w_a_up[l], LANES).astype(BF16),
            "w_g_up": w_g_up[l].astype(BF16),
            "rw_params": [heads(k_k[l]), heads(k_a[l]), heads(r_k[l]), heads(lnx_w[l]), heads(lnx_b[l])],
            "rw_rows": [a.reshape(1, d) for a in (k_k[l], k_a[l], r_k[l], lnx_w[l], lnx_b[l])],
            "q_norm_g": q_norm_g[l][None], "k_norm_g": k_norm_g[l][None],
            "w_proj_a": w_proj_a[l].astype(BF16), "w_proj_b": w_proj_b[l].astype(BF16),
            "w_out": w_out[l], "w_ffn_up": w_ffn_up[l],
            "conv_w": conv_w[l], "conv_b": conv_b[l][None], "w_ffn_down": w_ffn_down[l],
        }
        c_all = jnp.concatenate([c_prompt, c_sample], axis=0)
        rows = -(-c_all.shape[0] // SUBLANES) * SUBLANES
        mod = _matmul(pad_rows(c_all, rows), w_ada[l], tm=rows, tn=1024, bias=b_ada[l][None],
                      pre_silu=True, precise=True, name="ada_mod")
        mod_p, mod_s = mod[:bp], mod[bp:bp + bs]
        d_ff = w_ffn_down.shape[1]
        yp, *sp = _layer(yp, mod_p, jnp.zeros((bp, state_shift.shape[-1]), F32),
                         jnp.zeros((bp, n_rw, RW_HEAD, RW_HEAD), F32), jnp.zeros((bp, CONV_W - 1, d_ff), F32),
                         W, None)
        ys, *ss = _layer(ys, mod_s, state_shift[l], state_rwkv[l], state_conv[l], W,
                         (cache_k[l], cache_v[l], cache_kidx[l], page_table))
        st_p.append(sp)
        st_s.append(ss)
    stk = lambda sts, i: sts[0][i][None] if len(sts) == 1 else jnp.stack([s[i] for s in sts], axis=0)
    return (yp, ys,
            stk(st_p, 0), stk(st_p, 1), stk(st_p, 2), stk(st_p, 3), stk(st_p, 4), stk(st_p, 5),
            stk(st_s, 0), stk(st_s, 1), stk(st_s, 2), stk(st_s, 3), stk(st_s, 4), stk(st_s, 5))
```
